```python
import math
import jax, jax.numpy as jnp
from jax import lax
import numpy as np

D_MODEL = 1024
BATCH = 8
SEQ = 8192
DEPTH = 2
DEC_BATCH = 16
DEC_SEQ = 32
PAST_LEN = 4096

CHUNK = 64
EPS = 1e-6
N_EVEN = (DEPTH + 1) // 2
N_ODD = DEPTH // 2
MIX_WIDTH = D_MODEL
A_HEADS = 4
A_DK = D_MODEL // 8
A_DV = D_MODEL // 8
A_WIDTH = A_HEADS * A_DV
B_HEADS = 4
B_DK = D_MODEL // 8
B_DV = D_MODEL // 8
B_WIDTH = B_HEADS * B_DV
C_WIDTH = 3 * D_MODEL // 4
C_BLOCK = 64
C_BLOCKS = C_WIDTH // C_BLOCK
CONV_W = 4
RG_C = 8.0
S5_CH = 16
S5_STATE = 64
D_WIDTH = D_MODEL // 4
S5_GROUPS = D_WIDTH // S5_CH
FFN_DIM = ((8 * D_MODEL // 3 + 255) // 256) * 256
AB_SIZES = (A_WIDTH, A_WIDTH, A_WIDTH, A_WIDTH, A_HEADS, A_HEADS,
            B_HEADS * B_DK, B_HEADS * B_DK, B_WIDTH, B_WIDTH)
AB_IN = sum(AB_SIZES)
CD_SIZES = (C_WIDTH, C_WIDTH, D_WIDTH)
CD_IN = sum(CD_SIZES)

kernel_name = 'hybrid_streaming_mlstm_hgrn2_rglru_s5_step'


def rmsnorm(x, g):
    xf = x.astype(jnp.float32)
    y = xf * lax.rsqrt(jnp.mean(xf * xf, axis=-1, keepdims=True) + EPS)
    return (y * g.astype(jnp.float32)).astype(x.dtype)


def head_rmsnorm(h, g):
    bn, L, H, d = h.shape
    hf = h.astype(jnp.float32)
    y = hf * lax.rsqrt(jnp.mean(hf * hf, axis=-1, keepdims=True) + EPS)
    return y.reshape(bn, L, H * d) * g.astype(jnp.float32)


def swiglu(x, wg, wu, wd):
    return (jax.nn.silu(x @ wg) * (x @ wu)) @ wd


def _split(z, sizes):
    cuts = [int(c) for c in np.cumsum(sizes)[:-1]]
    return jnp.split(z, cuts, axis=-1)


def _chunk_len(L):
    return CHUNK if L % CHUNK == 0 else L


def _to_chunks(t, c):
    bn, L = t.shape[:2]
    return jnp.moveaxis(t.reshape((bn, L // c, c) + t.shape[2:]), 1, 0)


def _from_chunks(t):
    nc, bn, c = t.shape[:3]
    return jnp.moveaxis(t, 0, 1).reshape((bn, nc * c) + t.shape[3:])


def mlstm_chunkwise(q, k, v, ig, lf, C0, n0, m0):
    L = q.shape[1]
    c = _chunk_len(L)
    tril = jnp.tril(jnp.ones((c, c), dtype=bool))

    def step(carry, blk):
        C, n, m = carry
        qb, kb, vb, ib, fb = blk
        F = jnp.swapaxes(jnp.cumsum(fb, axis=1), 1, 2)
        it = jnp.swapaxes(ib, 1, 2)
        log_src = jnp.where(tril, F[..., :, None] - F[..., None, :] + it[..., None, :], -jnp.inf)
        log_prev = F + m[..., None]
        m_t = jnp.maximum(log_prev, jnp.max(log_src, axis=-1))
        w_src = jnp.exp(log_src - m_t[..., None])
        w_prev = jnp.exp(log_prev - m_t)
        s = jnp.einsum('bthd,bshd->bhts', qb, kb) * w_src
        num = (jnp.einsum('bhts,bshv->bthv', s, vb)
               + jnp.swapaxes(w_prev, 1, 2)[..., None] * jnp.einsum('bthd,bhdv->bthv', qb, C))
        den = jnp.sum(s, axis=-1) + w_prev * jnp.einsum('bthd,bhd->bht', qb, n)
        den = jnp.maximum(jnp.abs(den), jnp.exp(-m_t))
        h = num / jnp.swapaxes(den, 1, 2)[..., None]
        m_new = m_t[..., -1]
        decay_prev = jnp.exp(F[..., -1] + m - m_new)
        w_end = jnp.exp(F[..., -1:] - F + it - m_new[..., None])
        C = decay_prev[..., None, None] * C + jnp.einsum('bhs,bshd,bshv->bhdv', w_end, kb, vb)
        n = decay_prev[..., None] * n + jnp.einsum('bhs,bshd->bhd', w_end, kb)
        return (C, n, m_new), h

    blocks = (_to_chunks(q, c), _to_chunks(k, c), _to_chunks(v, c), _to_chunks(ig, c), _to_chunks(lf, c))
    (C, n, m), h = lax.scan(step, (C0, n0, m0), blocks)
    return _from_chunks(h), C, n, m


def mlstm_mixer(aq, ak, av, ao, ai, af, C0, n0, m0, norm_g):
    bn, L, _ = aq.shape
    f32 = jnp.float32
    q = aq.astype(f32).reshape(bn, L, A_HEADS, A_DK) * (A_DK ** -0.5)
    k = ak.astype(f32).reshape(bn, L, A_HEADS, A_DK)
    v = av.astype(f32).reshape(bn, L, A_HEADS, A_DV)
    ig = ai.astype(f32)
    lf = jax.nn.log_sigmoid(af.astype(f32))
    h, C, n, m = mlstm_chunkwise(q, k, v, ig, lf, C0.astype(f32), n0.astype(f32), m0.astype(f32))
    out = head_rmsnorm(h, norm_g) * jax.nn.sigmoid(ao.astype(f32))
    return out, C, n, m


def hgrn2_chunkwise(q, kk, lf, iv, S0):
    L = q.shape[1]
    c = _chunk_len(L)
    tril = jnp.tril(jnp.ones((c, c), dtype=bool))

    def step(S, blk):
        qb, kb, fb, ib = blk
        G = jnp.cumsum(fb, axis=1)
        q_dec = qb * jnp.exp(G)
        k_inv = kb * jnp.exp(-G)
        att = jnp.where(tril, jnp.einsum('bthd,bshd->bhts', q_dec, k_inv), 0.0)
        o = jnp.einsum('bhts,bshv->bthv', att, ib) + jnp.einsum('bthd,bhdv->bthv', q_dec, S)
        G_end = G[:, -1]
        k_end = kb * jnp.exp(G_end[:, None] - G)
        S = jnp.exp(G_end)[..., None] * S + jnp.einsum('bshd,bshv->bhdv', k_end, ib)
        return S, o

    blocks = (_to_chunks(q, c), _to_chunks(kk, c), _to_chunks(lf, c), _to_chunks(iv, c))
    S, o = lax.scan(step, S0, blocks)
    return _from_chunks(o), S


def hgrn2_mixer(bq, bf, bi, bg, S0, lb, norm_g):
    bn, L, _ = bq.shape
    f32 = jnp.float32
    q = jax.nn.silu(bq.astype(f32)).reshape(bn, L, B_HEADS, B_DK)
    zf = bf.astype(f32).reshape(bn, L, B_HEADS, B_DK)
    lbh = lb.reshape(B_HEADS, B_DK)
    lf = jnp.log(lbh + (1.0 - lbh) * jax.nn.sigmoid(zf))
    kk = (1.0 - lbh) * jax.nn.sigmoid(-zf)
    iv = bi.astype(f32).reshape(bn, L, B_HEADS, B_DV)
    o, S = hgrn2_chunkwise(q, kk, lf, iv, S0.astype(f32))
    out = head_rmsnorm(o, norm_g) * jax.nn.silu(bg.astype(f32))
    return out, S


def causal_dwconv(u, buf, w, b):
    L = u.shape[1]
    ext = jnp.concatenate([buf.astype(u.dtype), u], axis=1)
    y = b + sum(ext[:, j:j + L] * w[j] for j in range(CONV_W))
    return y, ext[:, ext.shape[1] - (CONV_W - 1):]


def _real_combine(l, r):
    return (l[0] * r[0], r[0] * l[1] + r[1])


def _complex_combine(l, r):
    lar, lai, lbr, lbi = l
    rar, rai, rbr, rbi = r
    return (rar * lar - rai * lai, rar * lai + rai * lar,
            rar * lbr - rai * lbi + rbr, rar * lbi + rai * lbr + rbi)


def rglru(u, h0, w_a, b_a, w_x, b_x, lam):
    bn, L, W = u.shape
    f32 = jnp.float32
    uf = u.astype(f32)
    ub = uf.reshape(bn, L, C_BLOCKS, C_BLOCK)
    r = jax.nn.sigmoid(jnp.einsum('blnd,nde->blne', ub, w_a.astype(f32)).reshape(bn, L, W) + b_a.astype(f32))
    i = jax.nn.sigmoid(jnp.einsum('blnd,nde->blne', ub, w_x.astype(f32)).reshape(bn, L, W) + b_x.astype(f32))
    log_a = RG_C * r * jax.nn.log_sigmoid(lam.astype(f32))
    a = jnp.exp(log_a)
    bterm = jnp.sqrt(-jnp.expm1(2.0 * log_a)) * (i * uf)
    bterm = bterm.at[:, 0].add(a[:, 0] * h0.astype(f32))
    _, h = lax.associative_scan(_real_combine, (a, bterm), axis=1)
    return h, h[:, -1]


def s5_mixer(u, h_re, h_im, A_re, A_im, log_dt, B_re, B_im, C_re, C_im, d_skip, w_glu, b_glu):
    bn, L, _ = u.shape
    f32 = jnp.float32
    uf = u.astype(f32).reshape(bn, L, S5_GROUPS, S5_CH)
    A_re = A_re.astype(f32)
    A_im = A_im.astype(f32)
    dt = jnp.exp(log_dt.astype(f32))[:, None]
    mag = jnp.exp(dt * A_re)
    ab_re = mag * jnp.cos(dt * A_im)
    ab_im = mag * jnp.sin(dt * A_im)
    inv = 1.0 / (A_re * A_re + A_im * A_im)
    z_re = ((ab_re - 1.0) * A_re + ab_im * A_im) * inv
    z_im = (ab_im * A_re - (ab_re - 1.0) * A_im) * inv
    B_re = B_re.astype(f32)
    B_im = B_im.astype(f32)
    bb_re = z_re[..., None] * B_re - z_im[..., None] * B_im
    bb_im = z_re[..., None] * B_im + z_im[..., None] * B_re
    bu_re = jnp.einsum('blgc,gpc->blgp', uf, bb_re)
    bu_im = jnp.einsum('blgc,gpc->blgp', uf, bb_im)
    h_re = h_re.astype(f32)
    h_im = h_im.astype(f32)
    bu_re = bu_re.at[:, 0].add(ab_re * h_re - ab_im * h_im)
    bu_im = bu_im.at[:, 0].add(ab_re * h_im + ab_im * h_re)
    shp = bu_re.shape
    _, _, s_re, s_im = lax.associative_scan(
        _complex_combine, (jnp.broadcast_to(ab_re, shp), jnp.broadcast_to(ab_im, shp), bu_re, bu_im), axis=1)
    y = (jnp.einsum('gcp,blgp->blgc', C_re.astype(f32), s_re)
         - jnp.einsum('gcp,blgp->blgc', C_im.astype(f32), s_im)
         + d_skip.astype(f32).reshape(S5_GROUPS, S5_CH) * uf)
    y = jax.nn.gelu(y.reshape(bn, L, D_WIDTH))
    y = y * jax.nn.sigmoid(y @ w_glu.astype(f32) + b_glu.astype(f32))
    return y, s_re[:, -1], s_im[:, -1]


def run_trunk(x, states, W):
    mC, mn, mm, hS, rh, rc, sre, sim = states
    f32 = jnp.float32
    lb_all = jnp.cumsum(jax.nn.softmax(W['hgrn_lb_logits'].astype(f32), axis=0), axis=0)
    outs = ([], [], [], [], [], [], [], [])
    for l in range(DEPTH):
        x = x + 0.5 * swiglu(rmsnorm(x, W['norm_g'][l, 0]), W['ffn_w_gate'][l, 0], W['ffn_w_up'][l, 0], W['ffn_w_down'][l, 0])
        xn = rmsnorm(x, W['norm_g'][l, 1])
        j = l // 2
        if l % 2 == 0:
            z = xn @ W['ab_w_in'][j] + W['ab_b_in'][j]
            aq, ak, av, ao, ai, af, bq, bf, bi, bg = _split(z, AB_SIZES)
            a_out, C1, n1, m1 = mlstm_mixer(aq, ak, av, ao, ai, af, mC[j], mn[j], mm[j], W['mlstm_norm_g'][j])
            b_out, S1 = hgrn2_mixer(bq, bf, bi, bg, hS[j], lb_all[l], W['hgrn_norm_g'][j])
            mix = jnp.concatenate([a_out, b_out], axis=-1) @ W['ab_w_out'][j]
            for lst, val in zip(outs[:4], (C1, n1, m1, S1)):
                lst.append(val)
        else:
            z = xn @ W['cd_w_in'][j] + W['cd_b_in'][j]
            cg, cr, du = _split(z, CD_SIZES)
            u, buf1 = causal_dwconv(cr, rc[j], W['conv_w'][j], W['conv_b'][j])
            h, h1 = rglru(u, rh[j], W['rg_w_a'][j], W['rg_b_a'][j], W['rg_w_x'][j], W['rg_b_x'][j], W['rg_lambda'][j])
            c_out = jax.nn.gelu(cg.astype(f32)) * h
            d_out, s1re, s1im = s5_mixer(du, sre[j], sim[j], W['s5_A_re'][j], W['s5_A_im'][j], W['s5_log_dt'][j],
                                         W['s5_B_re'][j], W['s5_B_im'][j], W['s5_C_re'][j], W['s5_C_im'][j],
                                         W['s5_D'][j], W['s5_w_glu'][j], W['s5_b_glu'][j])
            mix = jnp.concatenate([c_out, d_out], axis=-1) @ W['cd_w_out'][j]
            for lst, val in zip(outs[4:], (h1, buf1, s1re, s1im)):
                lst.append(val)
        x = x + mix.astype(x.dtype)
        x = x + 0.5 * swiglu(rmsnorm(x, W['norm_g'][l, 2]), W['ffn_w_gate'][l, 1], W['ffn_w_up'][l, 1], W['ffn_w_down'][l, 1])
    y = rmsnorm(x, W['final_norm_g'])
    return y, [jnp.stack(lst) for lst in outs]


def setup_inputs(seed: int = 0) -> dict:
    key = jax.random.key(seed)
    kit = iter(jax.random.split(key, 64))
    f32 = jnp.float32

    def nrm(shape, scale=1.0):
        return jax.random.normal(next(kit), shape, f32) * scale

    def unif(shape, lo, hi):
        return jax.random.uniform(next(kit), shape, f32, minval=lo, maxval=hi)

    x_prompt = nrm((BATCH, SEQ, D_MODEL))
    x_sample = nrm((DEC_BATCH, DEC_SEQ, D_MODEL))
    state_mlstm_C = nrm((N_EVEN, DEC_BATCH, A_HEADS, A_DK, A_DV), 0.5)
    state_mlstm_n = jnp.abs(nrm((N_EVEN, DEC_BATCH, A_HEADS, A_DK), 0.5))
    state_mlstm_m = nrm((N_EVEN, DEC_BATCH, A_HEADS))
    state_hgrn_S = nrm((N_EVEN, DEC_BATCH, B_HEADS, B_DK, B_DV), 0.5)
    state_rglru_h = nrm((N_ODD, DEC_BATCH, C_WIDTH), 0.5)
    state_rglru_conv = nrm((N_ODD, DEC_BATCH, CONV_W - 1, C_WIDTH))
    state_s5_re = nrm((N_ODD, DEC_BATCH, S5_GROUPS, S5_STATE), 0.1)
    state_s5_im = nrm((N_ODD, DEC_BATCH, S5_GROUPS, S5_STATE), 0.1)

    norm_g = 1.0 + nrm((DEPTH, 3, D_MODEL), 0.02)
    final_norm_g = 1.0 + nrm((D_MODEL,), 0.02)
    ffn_w_gate = nrm((DEPTH, 2, D_MODEL, FFN_DIM), D_MODEL ** -0.5)
    ffn_w_up = nrm((DEPTH, 2, D_MODEL, FFN_DIM), D_MODEL ** -0.5)
    ffn_w_down = nrm((DEPTH, 2, FFN_DIM, D_MODEL), FFN_DIM ** -0.5)

    ab_w_in = nrm((N_EVEN, D_MODEL, AB_IN), D_MODEL ** -0.5)
    f_off = 4 * A_WIDTH + A_HEADS
    ab_b_in = nrm((N_EVEN, AB_IN), 0.02).at[:, f_off:f_off + A_HEADS].add(jnp.linspace(3.0, 6.0, A_HEADS))
    mlstm_norm_g = 1.0 + nrm((N_EVEN, A_WIDTH), 0.02)
    hgrn_norm_g = 1.0 + nrm((N_EVEN, B_WIDTH), 0.02)
    hgrn_lb_logits = nrm((DEPTH + 1, B_HEADS * B_DK), 0.1)
    ab_w_out = nrm((N_EVEN, MIX_WIDTH, D_MODEL), MIX_WIDTH ** -0.5)

    cd_w_in = nrm((N_ODD, D_MODEL, CD_IN), D_MODEL ** -0.5)
    cd_b_in = nrm((N_ODD, CD_IN), 0.02)
    conv_w = nrm((N_ODD, CONV_W, C_WIDTH), CONV_W ** -0.5)
    conv_b = nrm((N_ODD, C_WIDTH), 0.02)
    rg_w_a = nrm((N_ODD, C_BLOCKS, C_BLOCK, C_BLOCK), C_BLOCK ** -0.5)
    rg_b_a = nrm((N_ODD, C_WIDTH), 0.02)
    rg_w_x = nrm((N_ODD, C_BLOCKS, C_BLOCK, C_BLOCK), C_BLOCK ** -0.5)
    rg_b_x = nrm((N_ODD, C_WIDTH), 0.02)
    a0 = unif((N_ODD, C_WIDTH), 0.9, 0.999)
    s0 = a0 ** (1.0 / RG_C)
    rg_lambda = jnp.log(s0) - jnp.log1p(-s0)

    s5_A_re = -0.5 + nrm((N_ODD, S5_GROUPS, S5_STATE), 0.01)
    s5_A_im = math.pi * jnp.arange(S5_STATE, dtype=f32) + nrm((N_ODD, S5_GROUPS, S5_STATE), 0.01)
    s5_log_dt = unif((N_ODD, S5_GROUPS), math.log(1e-3), math.log(1e-1))
    s5_B_re = nrm((N_ODD, S5_GROUPS, S5_STATE, S5_CH), (2 * S5_CH) ** -0.5)
    s5_B_im = nrm((N_ODD, S5_GROUPS, S5_STATE, S5_CH), (2 * S5_CH) ** -0.5)
    s5_C_re = nrm((N_ODD, S5_GROUPS, S5_CH, S5_STATE), (2 * S5_STATE) ** -0.5)
    s5_C_im = nrm((N_ODD, S5_GROUPS, S5_CH, S5_STATE), (2 * S5_STATE) ** -0.5)
    s5_D = nrm((N_ODD, D_WIDTH))
    s5_w_glu = nrm((N_ODD, D_WIDTH, D_WIDTH), D_WIDTH ** -0.5)
    s5_b_glu = nrm((N_ODD, D_WIDTH), 0.02)
    cd_w_out = nrm((N_ODD, MIX_WIDTH, D_MODEL), MIX_WIDTH ** -0.5)

    return {'x_prompt': x_prompt, 'x_sample': x_sample,
            'state_mlstm_C': state_mlstm_C, 'state_mlstm_n': state_mlstm_n, 'state_mlstm_m': state_mlstm_m,
            'state_hgrn_S': state_hgrn_S, 'state_rglru_h': state_rglru_h, 'state_rglru_conv': state_rglru_conv,
            'state_s5_re': state_s5_re, 'state_s5_im': state_s5_im,
            'norm_g': norm_g, 'final_norm_g': final_norm_g,
            'ffn_w_gate': ffn_w_gate, 'ffn_w_up': ffn_w_up, 'ffn_w_down': ffn_w_down,
            'ab_w_in': ab_w_in, 'ab_b_in': ab_b_in, 'mlstm_norm_g': mlstm_norm_g, 'hgrn_norm_g': hgrn_norm_g,
            'hgrn_lb_logits': hgrn_lb_logits, 'ab_w_out': ab_w_out,
            'cd_w_in': cd_w_in, 'cd_b_in': cd_b_in, 'conv_w': conv_w, 'conv_b': conv_b,
            'rg_w_a': rg_w_a, 'rg_b_a': rg_b_a, 'rg_w_x': rg_w_x, 'rg_b_x': rg_b_x, 'rg_lambda': rg_lambda,
            's5_A_re': s5_A_re, 's5_A_im': s5_A_im, 's5_log_dt': s5_log_dt,
            's5_B_re': s5_B_re, 's5_B_im': s5_B_im, 's5_C_re': s5_C_re, 's5_C_im': s5_C_im,
            's5_D': s5_D, 's5_w_glu': s5_w_glu, 's5_b_glu': s5_b_glu, 'cd_w_out': cd_w_out}


def reference(x_prompt, x_sample, state_mlstm_C, state_mlstm_n, state_mlstm_m, state_hgrn_S,
              state_rglru_h, state_rglru_conv, state_s5_re, state_s5_im,
              norm_g, final_norm_g, ffn_w_gate, ffn_w_up, ffn_w_down,
              ab_w_in, ab_b_in, mlstm_norm_g, hgrn_norm_g, hgrn_lb_logits, ab_w_out,
              cd_w_in, cd_b_in, conv_w, conv_b, rg_w_a, rg_b_a, rg_w_x, rg_b_x, rg_lambda,
              s5_A_re, s5_A_im, s5_log_dt, s5_B_re, s5_B_im, s5_C_re, s5_C_im, s5_D, s5_w_glu, s5_b_glu,
              cd_w_out):
    W = dict(norm_g=norm_g, final_norm_g=final_norm_g, ffn_w_gate=ffn_w_gate, ffn_w_up=ffn_w_up,
             ffn_w_down=ffn_w_down, ab_w_in=ab_w_in, ab_b_in=ab_b_in, mlstm_norm_g=mlstm_norm_g,
             hgrn_norm_g=hgrn_norm_g, hgrn_lb_logits=hgrn_lb_logits, ab_w_out=ab_w_out,
             cd_w_in=cd_w_in, cd_b_in=cd_b_in, conv_w=conv_w, conv_b=conv_b,
             rg_w_a=rg_w_a, rg_b_a=rg_b_a, rg_w_x=rg_w_x, rg_b_x=rg_b_x, rg_lambda=rg_lambda,
             s5_A_re=s5_A_re, s5_A_im=s5_A_im, s5_log_dt=s5_log_dt, s5_B_re=s5_B_re, s5_B_im=s5_B_im,
             s5_C_re=s5_C_re, s5_C_im=s5_C_im, s5_D=s5_D, s5_w_glu=s5_w_glu, s5_b_glu=s5_b_glu,
             cd_w_out=cd_w_out)
    f32 = jnp.float32
    bp = x_prompt.shape[0]
    zero_states = (jnp.zeros((N_EVEN, bp, A_HEADS, A_DK, A_DV), f32),
                   jnp.zeros((N_EVEN, bp, A_HEADS, A_DK), f32),
                   jnp.zeros((N_EVEN, bp, A_HEADS), f32),
                   jnp.zeros((N_EVEN, bp, B_HEADS, B_DK, B_DV), f32),
                   jnp.zeros((N_ODD, bp, C_WIDTH), f32),
                   jnp.zeros((N_ODD, bp, CONV_W - 1, C_WIDTH), x_prompt.dtype),
                   jnp.zeros((N_ODD, bp, S5_GROUPS, S5_STATE), f32),
                   jnp.zeros((N_ODD, bp, S5_GROUPS, S5_STATE), f32))
    y_prompt, p_st = run_trunk(x_prompt, zero_states, W)
    sample_states = (state_mlstm_C, state_mlstm_n, state_mlstm_m, state_hgrn_S,
                     state_rglru_h, state_rglru_conv, state_s5_re, state_s5_im)
    y_sample, s_st = run_trunk(x_sample, sample_states, W)
    p_mC, p_mn, p_mm, p_hS, p_rh, p_rc, p_sre, p_sim = p_st
    s_mC, s_mn, s_mm, s_hS, s_rh, s_rc, s_sre, s_sim = s_st
    return (y_prompt, y_sample, p_mC, p_mn, p_mm, p_hS, p_rh, p_rc, p_sre, p_sim,
            s_mC, s_mn, s_mm, s_hS, s_rh, s_rc, s_sre, s_sim)
```

```python
import functools
import math

import jax
import jax.numpy as jnp
from jax import lax
from jax.experimental import pallas as pl
from jax.experimental.pallas import tpu as pltpu

F32 = jnp.float32
BF16 = jnp.bfloat16

EPS = 1e-6
RG_C = 8.0
CONV_W = 4
HEADS = 4
HEAD_DIM = 128
N_GATE_LANES = 128
N_GATE_ROWS = 16
V7X_VMEM_LIMIT = 56 * 1024 * 1024
FFN_COLS = 256
FFN_ROWS = 512
MIX_ROWS = 256
HGRN_CHUNK = 64
MAX_SCAN_STEPS = 8


def _dot(a, b):
    return jnp.dot(a, b, preferred_element_type=F32)


def _dot_nt(a, b):
    return lax.dot_general(a, b, (((1,), (1,)), ((), ())), preferred_element_type=F32)


def _dot_tn(a, b):
    return lax.dot_general(a, b, (((0,), (0,)), ((), ())), preferred_element_type=F32)


def _rms(x, g):
    y = x * lax.rsqrt(jnp.mean(x * x, axis=-1, keepdims=True) + EPS)
    return y * g


def _sigmoid(x):
    return jax.nn.sigmoid(x)


def _log_sigmoid(x):
    return jnp.minimum(x, 0.0) - jnp.log1p(jnp.exp(-jnp.abs(x)))


def _split3(a):
    hi = a.astype(BF16)
    r = a - hi.astype(F32)
    mid = r.astype(BF16)
    lo = (r - mid.astype(F32)).astype(BF16)
    return hi, mid, lo


def _const_spec(shape):
    nd = len(shape)
    return pl.BlockSpec(shape, lambda *_: (0,) * nd, pipeline_mode=pl.Buffered(1))


def _cparams(sem):
    return pltpu.CompilerParams(dimension_semantics=sem, vmem_limit_bytes=V7X_VMEM_LIMIT)


def _ffn_kernel(x_ref, g_ref, wg_ref, wu_ref, wd_ref, fg_ref, o_ref, acc_ref, *, n_chunks, final_norm):
    x = x_ref[...]
    xn = _rms(x, g_ref[...]).astype(BF16)
    for j in range(n_chunks):
        gate = _dot(xn, wg_ref[j])
        up = _dot(xn, wu_ref[j])
        h = ((gate * _sigmoid(gate)) * up).astype(BF16)
        d = _dot(h, wd_ref[j])
        if j == 0:
            acc_ref[...] = d
        else:
            acc_ref[...] += d
    y = x + 0.5 * acc_ref[...]
    if final_norm:
        y = _rms(y, fg_ref[...])
    o_ref[...] = y


def _ffn(x2, g, wg, wu, wd, fg, final_norm):
    n_tok, d = x2.shape
    tm = min(FFN_ROWS, n_tok)
    n_chunks = wg.shape[0]
    kern = functools.partial(_ffn_kernel, n_chunks=n_chunks, final_norm=final_norm)
    return pl.pallas_call(
        kern,
        grid=(n_tok // tm,),
        in_specs=[pl.BlockSpec((tm, d), lambda i: (i, 0)),
                  _const_spec(g.shape), _const_spec(wg.shape), _const_spec(wu.shape),
                  _const_spec(wd.shape), _const_spec(fg.shape)],
        out_specs=pl.BlockSpec((tm, d), lambda i: (i, 0)),
        out_shape=jax.ShapeDtypeStruct((n_tok, d), F32),
        scratch_shapes=[pltpu.VMEM((tm, d), F32)],
        compiler_params=_cparams(("parallel",)),
        name="ffn",
    )(x2, g, wg, wu, wd, fg)


def _mixer_ab_kernel(x_ref, ng_ref, wbig_ref, bbig_ref, wgc_ref, bgc_ref, wgr_ref, bgr_ref,
                     ang_ref, bng_ref, lb_ref, wout_ref, c0_ref, n0_ref, m0_ref, s0_ref,
                     xo_ref, c_ref, n_ref, m_ref, s_ref, st_ref, hmix_ref, *, rows, hchunk):
    T = rows
    W = HEADS * HEAD_DIM
    li = pl.program_id(1)

    @pl.when(li == 0)
    def _():
        c_ref[...] = c0_ref[...]
        n_ref[...] = n0_ref[...]
        m_ref[...] = m0_ref[...]
        for h in range(HEADS):
            st_ref[h] = s0_ref[h].T

    x = x_ref[...]
    xn = _rms(x, ng_ref[...]).astype(BF16)

    def proj(k):
        return _dot(xn, wbig_ref[:, k * W:(k + 1) * W]) + bbig_ref[:, k * W:(k + 1) * W]

    aq = proj(0) * (HEAD_DIM ** -0.5)
    ak = proj(1)
    av = proj(2)
    ao = proj(3)
    zc = _dot(xn, wgc_ref[...]) + bgc_ref[...]
    zr = _dot_nt(wgr_ref[...], xn) + bgr_ref[...]
    ri = lax.broadcasted_iota(jnp.int32, (T, T), 0)
    ci = lax.broadcasted_iota(jnp.int32, (T, T), 1)
    causal = ri >= ci
    tril = jnp.where(causal, 1.0, 0.0).astype(BF16)
    triu = jnp.where(ri <= ci, 1.0, 0.0).astype(BF16)
    fc = sum(_dot(tril, p) for p in _split3(_log_sigmoid(zc)))
    fr = sum(_dot(p, triu) for p in _split3(_log_sigmoid(zr)))
    for h in range(HEADS):
        cols = slice(h * HEAD_DIM, (h + 1) * HEAD_DIM)
        ig_c = zc[:, h:h + 1]
        ig_r = zr[h:h + 1, :]
        f_c = fc[:, HEADS + h:HEADS + h + 1]
        f_r = fr[HEADS + h:HEADS + h + 1, :]
        m_prev = m_ref[h:h + 1, 0:1]
        log_src = jnp.where(causal, f_c + (ig_r - f_r), -jnp.inf)
        log_prev = f_c + m_prev
        m_t = jnp.maximum(log_prev, jnp.max(log_src, axis=-1, keepdims=True))
        w_src = jnp.exp(log_src - m_t)
        w_prev = jnp.exp(log_prev - m_t)
        qf = aq[:, cols]
        kf = ak[:, cols]
        q = qf.astype(BF16)
        v = av[:, cols].astype(BF16)
        s = _dot_nt(q, kf.astype(BF16)) * w_src
        c_old = c_ref[h]
        n_old = n_ref[h:h + 1, :]
        num = _dot(s.astype(BF16), v) + w_prev * _dot(q, c_old.astype(BF16))
        den = jnp.sum(s, axis=-1, keepdims=True) + w_prev * jnp.sum(qf * n_old, axis=-1, keepdims=True)
        den = jnp.maximum(jnp.abs(den), jnp.exp(-m_t))
        hh = num / den
        f_last = f_c[T - 1:T, :]
        m_new = m_t[T - 1:T, :]
        decay = jnp.exp(f_last + m_prev - m_new)
        w_end = jnp.exp(f_last - f_c + ig_c - m_new)
        wk = w_end * kf
        c_ref[h] = decay * c_old + _dot_tn(wk.astype(BF16), v)
        n_ref[h:h + 1, :] = decay * n_old + jnp.sum(wk, axis=0, keepdims=True)
        m_ref[h:h + 1, :] = jnp.broadcast_to(m_new, (1, N_GATE_LANES))
        y = hh * lax.rsqrt(jnp.mean(hh * hh, axis=-1, keepdims=True) + EPS)
        out = (y * ang_ref[:, cols]) * _sigmoid(ao[:, cols])
        hmix_ref[:, cols] = out.astype(BF16)

    qh = proj(4)
    qh = qh * _sigmoid(qh)
    zf = proj(5)
    iv = proj(6)
    bg = proj(7)
    lb = lb_ref[...]
    lf = jnp.log(lb + (1.0 - lb) * _sigmoid(zf))
    kk = (1.0 - lb) * _sigmoid(-zf)
    sh = int(math.log2(hchunk))
    same_chunk = lax.shift_right_logical(ri, sh) == lax.shift_right_logical(ci, sh)
    blocktril = jnp.where(causal & same_chunk, 1.0, 0.0).astype(BF16)
    G = sum(_dot(blocktril, p) for p in _split3(lf))
    q_dec = (qh * jnp.exp(G)).astype(BF16)
    k_inv = (kk * jnp.exp(-G)).astype(BF16)
    ivb = iv.astype(BF16)
    rs = lax.broadcasted_iota(jnp.int32, (hchunk, hchunk), 0)
    cs = lax.broadcasted_iota(jnp.int32, (hchunk, hchunk), 1)
    sub_causal = rs >= cs
    for j in range(T // hchunk):
        r = slice(j * hchunk, (j + 1) * hchunk)
        g_end = G[(j + 1) * hchunk - 1:(j + 1) * hchunk, :]
        k_end = (kk[r, :] * jnp.exp(g_end - G[r, :])).astype(BF16)
        e_end = jnp.exp(g_end)
        for h in range(HEADS):
            cols = slice(h * HEAD_DIM, (h + 1) * HEAD_DIM)
            qd = q_dec[r, cols]
            att = jnp.where(sub_causal, _dot_nt(qd, k_inv[r, cols]), 0.0)
            st_old = st_ref[h]
            o = _dot(att.astype(BF16), ivb[r, cols]) + _dot_nt(qd, st_old.astype(BF16))
            st_ref[h] = st_old * e_end[:, cols] + _dot_tn(ivb[r, cols], k_end[:, cols])
            y = o * lax.rsqrt(jnp.mean(o * o, axis=-1, keepdims=True) + EPS)
            gate = bg[r, cols]
            out = (y * bng_ref[:, cols]) * (gate * _sigmoid(gate))
            hmix_ref[r, W + h * HEAD_DIM:W + (h + 1) * HEAD_DIM] = out.astype(BF16)

    xo_ref[...] = x + _dot(hmix_ref[...], wout_ref[...])

    @pl.when(li == pl.num_programs(1) - 1)
    def _():
        for h in range(HEADS):
            s_ref[h] = st_ref[h].T


def _mixer_ab(x, ng, w, c0, n0, m0, s0):
    B, L, D = x.shape
    T = min(MIX_ROWS, L)
    hchunk = HGRN_CHUNK if T % HGRN_CHUNK == 0 else T
    W = HEADS * HEAD_DIM
    kern = functools.partial(_mixer_ab_kernel, rows=T, hchunk=hchunk)
    state4 = pl.BlockSpec((None, HEADS, HEAD_DIM, HEAD_DIM), lambda b, l: (b, 0, 0, 0))
    state_n = pl.BlockSpec((None, HEADS, HEAD_DIM), lambda b, l: (b, 0, 0))
    state_m = pl.BlockSpec((None, 8, N_GATE_LANES), lambda b, l: (b, 0, 0))
    xspec = pl.BlockSpec((None, T, D), lambda b, l: (b, l, 0))
    consts = (ng, w["w_big"], w["b_big"], w["w_gc"], w["b_gc"], w["w_gr"], w["b_gr"],
              w["a_norm_g"], w["b_norm_g"], w["lb"], w["w_out"])
    return pl.pallas_call(
        kern,
        grid=(B, L // T),
        in_specs=[xspec] + [_const_spec(c.shape) for c in consts] + [state4, state_n, state_m, state4],
        out_specs=[xspec, state4, state_n, state_m, state4],
        out_shape=[jax.ShapeDtypeStruct((B, L, D), F32),
                   jax.ShapeDtypeStruct((B, HEADS, HEAD_DIM, HEAD_DIM), F32),
                   jax.ShapeDtypeStruct((B, HEADS, HEAD_DIM), F32),
                   jax.ShapeDtypeStruct((B, 8, N_GATE_LANES), F32),
                   jax.ShapeDtypeStruct((B, HEADS, HEAD_DIM, HEAD_DIM), F32)],
        scratch_shapes=[pltpu.VMEM((HEADS, HEAD_DIM, HEAD_DIM), F32),
                        pltpu.VMEM((T, 2 * W), BF16)],
        compiler_params=_cparams(("parallel", "arbitrary")),
        name="mixer_ab",
    )(x, *consts, c0, n0, m0, s0)


def _mixer_cd_kernel(x_ref, ng_ref, wcd_ref, bcd_ref, cw_ref, cb_ref, wa_ref, ba_ref, wx_ref, bx_ref,
                     lam_ref, wbu_ref, abr_ref, abi_ref, pr_ref, pi_ref, wc_ref, dsk_ref, wglu_ref, bglu_ref,
                     wout_ref, h0_ref, conv0_ref, sre0_ref, sim0_ref,
                     xo_ref, h_ref, conv_ref, sre_ref, sim_ref, ext_ref, hmix_ref,
                     *, rows, c_width, d_width, n_state, gate_tile):
    T = rows
    li = pl.program_id(1)
    pad = 8

    @pl.when(li == 0)
    def _():
        h_ref[...] = h0_ref[...]
        sre_ref[...] = sre0_ref[...]
        sim_ref[...] = sim0_ref[...]
        ext_ref[pad - (CONV_W - 1):pad, :] = conv0_ref[...]

    x = x_ref[...]
    xn = _rms(x, ng_ref[...]).astype(BF16)

    def proj(lo, hi):
        return _dot(xn, wcd_ref[:, lo:hi]) + bcd_ref[:, lo:hi]

    cg = proj(0, c_width)
    cr = proj(c_width, 2 * c_width)
    du = proj(2 * c_width, 2 * c_width + d_width)

    ext_ref[pad:pad + T, :] = cr
    u = cb_ref[...] + cw_ref[CONV_W - 1:CONV_W, :] * cr
    for j in range(CONV_W - 1):
        u = u + cw_ref[j:j + 1, :] * ext_ref[pad - (CONV_W - 1) + j:pad - (CONV_W - 1) + j + T, :]
    tail = ext_ref[pad + T - (CONV_W - 1):pad + T, :]
    conv_ref[...] = tail
    ext_ref[pad - (CONV_W - 1):pad, :] = tail

    ub = u.astype(BF16)
    log_sig_lam = _log_sigmoid(lam_ref[...])
    steps = [1 << k for k in range(int(math.log2(T)))]
    for k in range(c_width // gate_tile):
        cols = slice(k * gate_tile, (k + 1) * gate_tile)
        row = lax.broadcasted_iota(jnp.int32, (T, gate_tile), 0)
        r = _sigmoid(_dot(ub[:, cols], wa_ref[k]) + ba_ref[:, cols])
        i = _sigmoid(_dot(ub[:, cols], wx_ref[k]) + bx_ref[:, cols])
        log_a = (RG_C * r) * log_sig_lam[:, cols]
        a = jnp.exp(log_a)
        b = jnp.sqrt(-jnp.tanh(log_a) * (a * a + 1.0)) * (i * u[:, cols])
        b = b + jnp.where(row == 0, a * h_ref[:, cols], 0.0)
        for d in steps:
            a_sh = jnp.where(row >= d, pltpu.roll(a, d, 0), 1.0)
            b_sh = jnp.where(row >= d, pltpu.roll(b, d, 0), 0.0)
            b = a * b_sh + b
            a = a * a_sh
        h_ref[:, cols] = b[T - 1:T, :]
        hmix_ref[:, cols] = (jax.nn.gelu(cg[:, cols], approximate=True) * b).astype(BF16)

    bu = _dot(du.astype(BF16), wbu_ref[...])
    re = bu[:, :n_state]
    im = bu[:, n_state:]
    row = lax.broadcasted_iota(jnp.int32, (T, n_state), 0)
    abr = abr_ref[...]
    abi = abi_ref[...]
    s_re = sre_ref[...]
    s_im = sim_ref[...]
    re = re + jnp.where(row == 0, abr * s_re - abi * s_im, 0.0)
    im = im + jnp.where(row == 0, abr * s_im + abi * s_re, 0.0)
    for k, d in enumerate(steps):
        p_re = pr_ref[k:k + 1, :]
        p_im = pi_ref[k:k + 1, :]
        re_sh = jnp.where(row >= d, pltpu.roll(re, d, 0), 0.0)
        im_sh = jnp.where(row >= d, pltpu.roll(im, d, 0), 0.0)
        re, im = re + (p_re * re_sh - p_im * im_sh), im + (p_re * im_sh + p_im * re_sh)
    sre_ref[...] = re[T - 1:T, :]
    sim_ref[...] = im[T - 1:T, :]
    y = (_dot(re.astype(BF16), wc_ref[:n_state, :]) + _dot(im.astype(BF16), wc_ref[n_state:, :])
         + dsk_ref[...] * du)
    y = jax.nn.gelu(y, approximate=True)
    y = y * _sigmoid(_dot(y.astype(BF16), wglu_ref[...]) + bglu_ref[...])
    hmix_ref[:, c_width:c_width + d_width] = y.astype(BF16)

    xo_ref[...] = x + _dot(hmix_ref[...], wout_ref[...])


def _mixer_cd(x, ng, w, h0, conv0, sre0, sim0):
    B, L, D = x.shape
    T = min(MIX_ROWS, L)
    c_width = h0.shape[-1]
    n_state = sre0.shape[-1]
    d_width = w["w_glu"].shape[0]
    kern = functools.partial(_mixer_cd_kernel, rows=T, c_width=c_width, d_width=d_width, n_state=n_state,
                             gate_tile=w["w_a"].shape[-1])
    xspec = pl.BlockSpec((None, T, D), lambda b, l: (b, l, 0))
    st_h = pl.BlockSpec((None, 1, c_width), lambda b, l: (b, 0, 0))
    st_conv = pl.BlockSpec((None, CONV_W - 1, c_width), lambda b, l: (b, 0, 0))
    st_s = pl.BlockSpec((None, 1, n_state), lambda b, l: (b, 0, 0))
    consts = (ng, w["w_cd"], w["b_cd"], w["conv_w"], w["conv_b"], w["w_a"], w["b_a"], w["w_x"], w["b_x"],
              w["lam"], w["w_bu"], w["ab_re"], w["ab_im"], w["pow_re"], w["pow_im"], w["w_c"], w["d_skip"],
              w["w_glu"], w["b_glu"], w["w_out"])
    return pl.pallas_call(
        kern,
        grid=(B, L // T),
        in_specs=[xspec] + [_const_spec(c.shape) for c in consts] + [st_h, st_conv, st_s, st_s],
        out_specs=[xspec, st_h, st_conv, st_s, st_s],
        out_shape=[jax.ShapeDtypeStruct((B, L, D), F32),
                   jax.ShapeDtypeStruct((B, 1, c_width), F32),
                   jax.ShapeDtypeStruct((B, CONV_W - 1, c_width), F32),
                   jax.ShapeDtypeStruct((B, 1, n_state), F32),
                   jax.ShapeDtypeStruct((B, 1, n_state), F32)],
        scratch_shapes=[pltpu.VMEM((T + 8, c_width), F32),
                        pltpu.VMEM((T, c_width + d_width), BF16)],
        compiler_params=_cparams(("parallel", "arbitrary")),
        name="mixer_cd",
    )(x, *consts, h0, conv0, sre0, sim0)


def _prep_s5_kernel(are_ref, aim_ref, ldt_ref, wbre_ref, wbim_ref, abr_ref, abi_ref, pr_ref, pi_ref, wbu_ref,
                    *, n_state):
    a_re = are_ref[...]
    a_im = aim_ref[...]
    dt = jnp.exp(ldt_ref[...])
    mag = jnp.exp(dt * a_re)
    ab_re = mag * jnp.cos(dt * a_im)
    ab_im = mag * jnp.sin(dt * a_im)
    inv = 1.0 / (a_re * a_re + a_im * a_im)
    z_re = ((ab_re - 1.0) * a_re + ab_im * a_im) * inv
    z_im = (ab_im * a_re - (ab_re - 1.0) * a_im) * inv
    abr_ref[...] = ab_re
    abi_ref[...] = ab_im
    wb_re = wbre_ref[...]
    wb_im = wbim_ref[...]
    wbu_ref[:, :n_state] = (z_re * wb_re - z_im * wb_im).astype(BF16)
    wbu_ref[:, n_state:] = (z_re * wb_im + z_im * wb_re).astype(BF16)
    p_re, p_im = ab_re, ab_im
    for k in range(MAX_SCAN_STEPS):
        pr_ref[k:k + 1, :] = p_re
        pi_ref[k:k + 1, :] = p_im
        p_re, p_im = p_re * p_re - p_im * p_im, 2.0 * (p_re * p_im)


def _prep_s5(a_re, a_im, log_dt_rep, wb_re, wb_im):
    n_state = a_re.shape[-1]
    d_width = wb_re.shape[0]
    kern = functools.partial(_prep_s5_kernel, n_state=n_state)
    row = jax.ShapeDtypeStruct((1, n_state), F32)
    tab = jax.ShapeDtypeStruct((MAX_SCAN_STEPS, n_state), F32)
    return pl.pallas_call(
        kern,
        out_shape=[row, row, tab, tab, jax.ShapeDtypeStruct((d_width, 2 * n_state), BF16)],
        name="prep_s5",
    )(a_re, a_im, log_dt_rep, wb_re, wb_im)


def _prep_lb_kernel(logits_ref, lb_ref):
    lg = logits_ref[...]
    n = lg.shape[0]
    mx = lg[0:1, :]
    for i in range(1, n):
        mx = jnp.maximum(mx, lg[i:i + 1, :])
    e = [jnp.exp(lg[i:i + 1, :] - mx) for i in range(n)]
    tot = e[0]
    for i in range(1, n):
        tot = tot + e[i]
    run = None
    for i in range(n):
        p = e[i] / tot
        run = p if run is None else run + p
        lb_ref[i:i + 1, :] = run


def _prep_lb(logits):
    return pl.pallas_call(_prep_lb_kernel, out_shape=jax.ShapeDtypeStruct(logits.shape, F32),
                          name="prep_lb")(logits)


def _block_diag(blocks):
    n, r, c = blocks.shape
    eye = jnp.eye(n, dtype=blocks.dtype)
    return (eye[:, None, :, None] * blocks[:, :, None, :]).reshape(n * r, n * c)


def _chunk_cols(w, cols):
    k, n = w.shape
    return w.reshape(k, n // cols, cols).transpose(1, 0, 2)


def kernel(x_prompt, x_sample, state_mlstm_C, state_mlstm_n, state_mlstm_m, state_hgrn_S, state_rglru_h,
           state_rglru_conv, state_s5_re, state_s5_im, norm_g, final_norm_g, ffn_w_gate, ffn_w_up, ffn_w_down,
           ab_w_in, ab_b_in, mlstm_norm_g, hgrn_norm_g, hgrn_lb_logits, ab_w_out, cd_w_in, cd_b_in, conv_w,
           conv_b, rg_w_a, rg_b_a, rg_w_x, rg_b_x, rg_lambda, s5_A_re, s5_A_im, s5_log_dt, s5_B_re, s5_B_im,
           s5_C_re, s5_C_im, s5_D, s5_w_glu, s5_b_glu, cd_w_out):
    depth = norm_g.shape[0]
    d_model = x_prompt.shape[-1]
    W = HEADS * HEAD_DIM
    c_width = rg_lambda.shape[-1]
    groups, n_st, s5_ch = s5_B_re.shape[1:]
    n_state = groups * n_st
    d_width = groups * s5_ch
    gate_tile = 256
    blocks_per_tile = gate_tile // rg_w_a.shape[-1]

    lb_all = _prep_lb(hgrn_lb_logits.astype(F32))

    ffn = []
    for l in range(depth):
        ffn.append([(_chunk_cols(ffn_w_gate[l, i].astype(BF16), FFN_COLS),
                     _chunk_cols(ffn_w_up[l, i].astype(BF16), FFN_COLS),
                     ffn_w_down[l, i].astype(BF16).reshape(-1, FFN_COLS, d_model)) for i in range(2)])
    mix_w = []
    for l in range(depth):
        j = l // 2
        if l % 2 == 0:
            wi, bi = ab_w_in[j], ab_b_in[j]
            big = jnp.concatenate([wi[:, :4 * W], wi[:, 4 * W + 2 * HEADS:]], axis=1)
            b_big = jnp.concatenate([bi[:4 * W], bi[4 * W + 2 * HEADS:]])[None, :]
            wg = wi[:, 4 * W:4 * W + 2 * HEADS]
            bgate = bi[4 * W:4 * W + 2 * HEADS]
            mix_w.append(dict(
                w_big=big.astype(BF16), b_big=b_big,
                w_gc=jnp.pad(wg, ((0, 0), (0, N_GATE_LANES - 2 * HEADS))).astype(BF16),
                b_gc=jnp.pad(bgate, (0, N_GATE_LANES - 2 * HEADS))[None, :],
                w_gr=jnp.pad(wg.T, ((0, N_GATE_ROWS - 2 * HEADS), (0, 0))).astype(BF16),
                b_gr=jnp.pad(bgate, (0, N_GATE_ROWS - 2 * HEADS))[:, None],
                a_norm_g=mlstm_norm_g[j][None, :], b_norm_g=hgrn_norm_g[j][None, :],
                lb=lb_all[l][None, :], w_out=ab_w_out[j].astype(BF16)))
        else:
            n_tiles = c_width // gate_tile

            def tiles(wb):
                wb = wb.reshape(n_tiles, blocks_per_tile, wb.shape[-2], wb.shape[-1])
                return jnp.stack([_block_diag(wb[t]) for t in range(n_tiles)]).astype(BF16)

            wb_re = _block_diag(jnp.swapaxes(s5_B_re[j], 1, 2))
            wb_im = _block_diag(jnp.swapaxes(s5_B_im[j], 1, 2))
            ab_re, ab_im, pow_re, pow_im, w_bu = _prep_s5(
                s5_A_re[j].reshape(1, n_state), s5_A_im[j].reshape(1, n_state),
                jnp.repeat(s5_log_dt[j], n_st)[None, :], wb_re, wb_im)
            w_c = jnp.concatenate([_block_diag(jnp.swapaxes(s5_C_re[j], 1, 2)),
                                   -_block_diag(jnp.swapaxes(s5_C_im[j], 1, 2))], axis=0)
            mix_w.append(dict(
                w_cd=cd_w_in[j].astype(BF16), b_cd=cd_b_in[j][None, :], conv_w=conv_w[j], conv_b=conv_b[j][None, :],
                w_a=tiles(rg_w_a[j]), b_a=rg_b_a[j][None, :], w_x=tiles(rg_w_x[j]), b_x=rg_b_x[j][None, :],
                lam=rg_lambda[j][None, :], w_bu=w_bu, ab_re=ab_re, ab_im=ab_im, pow_re=pow_re, pow_im=pow_im,
                w_c=w_c.astype(BF16), d_skip=s5_D[j][None, :], w_glu=s5_w_glu[j].astype(BF16),
                b_glu=s5_b_glu[j][None, :], w_out=cd_w_out[j].astype(BF16)))
    fg = final_norm_g[None, :]

    def run_trunk(x, states):
        mC, mn, mm, hS, rh, rc, sre, sim = states
        B, L, _ = x.shape
        outs = ([], [], [], [], [], [], [], [])
        for l in range(depth):
            j = l // 2
            wg, wu, wd = ffn[l][0]
            x = _ffn(x.reshape(B * L, d_model), norm_g[l, 0][None, :], wg, wu, wd, fg, False).reshape(B, L, d_model)
            if l % 2 == 0:
                m0 = jnp.broadcast_to(jnp.pad(mm[j], ((0, 0), (0, 8 - HEADS)))[:, :, None], (B, 8, N_GATE_LANES))
                x, c1, n1, m1, s1 = _mixer_ab(x, norm_g[l, 1][None, :], mix_w[l], mC[j], mn[j], m0, hS[j])
                for lst, val in zip(outs[:4], (c1, n1, m1[:, :HEADS, 0], s1)):
                    lst.append(val)
            else:
                x, h1, buf1, re1, im1 = _mixer_cd(x, norm_g[l, 1][None, :], mix_w[l], rh[j][:, None, :], rc[j],
                                                  sre[j].reshape(B, 1, n_state), sim[j].reshape(B, 1, n_state))
                for lst, val in zip(outs[4:], (h1[:, 0], buf1, re1.reshape(B, groups, n_st),
                                               im1.reshape(B, groups, n_st))):
                    lst.append(val)
            wg, wu, wd = ffn[l][1]
            x = _ffn(x.reshape(B * L, d_model), norm_g[l, 2][None, :], wg, wu, wd, fg,
                     l == depth - 1).reshape(B, L, d_model)
        return x, [jnp.stack(lst) for lst in outs]

    n_even = (depth + 1) // 2
    n_odd = depth // 2
    bp = x_prompt.shape[0]
    zero_states = (jnp.zeros((n_even, bp, HEADS, HEAD_DIM, HEAD_DIM), F32),
                   jnp.zeros((n_even, bp, HEADS, HEAD_DIM), F32),
                   jnp.zeros((n_even, bp, HEADS), F32),
                   jnp.zeros((n_even, bp, HEADS, HEAD_DIM, HEAD_DIM), F32),
                   jnp.zeros((n_odd, bp, c_width), F32),
                   jnp.zeros((n_odd, bp, CONV_W - 1, c_width), F32),
                   jnp.zeros((n_odd, bp, groups, n_st), F32),
                   jnp.zeros((n_odd, bp, groups, n_st), F32))
    y_prompt, p_st = run_trunk(x_prompt, zero_states)
    y_sample, s_st = run_trunk(x_sample, (state_mlstm_C, state_mlstm_n, state_mlstm_m, state_hgrn_S,
                                          state_rglru_h, state_rglru_conv, state_s5_re, state_s5_im))
    return (y_prompt, y_sample, *p_st, *s_st)
```

```python
import functools
import math

import jax
import jax.numpy as jnp
import numpy as np
from jax import lax
from jax.experimental import pallas as pl
from jax.experimental.pallas import tpu as pltpu

F32 = jnp.float32
BF16 = jnp.bfloat16

EPS = 1e-6
RG_C = 8.0
CONV_W = 4
HEADS = 4
HEAD_DIM = 128
N_GATE_LANES = 128
N_GATE_ROWS = 16
V7X_VMEM_LIMIT = 56 * 1024 * 1024
FFN_COLS = 256
FFN_ROWS = 512
MIX_ROWS = 256
HGRN_CHUNK = 64
MAX_SCAN_STEPS = 8
MIX_SEQS = 4
CD_PHASE_PIECES = 8


def _dot(a, b):
    return jnp.dot(a, b, preferred_element_type=F32)


def _dot_nt(a, b):
    return lax.dot_general(a, b, (((1,), (1,)), ((), ())), preferred_element_type=F32)


def _dot_tn(a, b):
    return lax.dot_general(a, b, (((0,), (0,)), ((), ())), preferred_element_type=F32)


def _rms(x, g):
    y = x * lax.rsqrt(jnp.mean(x * x, axis=-1, keepdims=True) + EPS)
    return y * g


def _sigmoid(x):
    return jax.nn.sigmoid(x)


def _log_sigmoid(x):
    return jnp.minimum(x, 0.0) - jnp.log1p(jnp.exp(-jnp.abs(x)))


def _split3(a):
    hi = a.astype(BF16)
    r = a - hi.astype(F32)
    mid = r.astype(BF16)
    lo = (r - mid.astype(F32)).astype(BF16)
    return hi, mid, lo


def _const_spec(shape):
    nd = len(shape)
    return pl.BlockSpec(shape, lambda *_: (0,) * nd, pipeline_mode=pl.Buffered(1))


def _run_staggered(piece_generators, offset):
    pending = list(piece_generators)
    live = []
    tick = 0
    while pending or live:
        if pending and tick % offset == 0:
            live.append(pending.pop(0))
        for g in list(live):
            try:
                next(g)
            except StopIteration:
                live.remove(g)
        tick += 1


def _cparams(sem):
    return pltpu.CompilerParams(dimension_semantics=sem, vmem_limit_bytes=V7X_VMEM_LIMIT)


def _ffn_kernel(x_ref, g_ref, wg_ref, wu_ref, wd_ref, fg_ref, o_ref, acc_ref, *, n_chunks, final_norm):
    x = x_ref[...]
    xn = _rms(x, g_ref[...]).astype(BF16)
    for j in range(n_chunks):
        cols = slice(j * FFN_COLS, (j + 1) * FFN_COLS)
        gate = _dot(xn, wg_ref[:, cols])
        up = _dot(xn, wu_ref[:, cols])
        h = ((gate * _sigmoid(gate)) * up).astype(BF16)
        d = _dot(h, wd_ref[cols, :])
        if j == 0:
            acc_ref[...] = d
        else:
            acc_ref[...] += d
    y = x + 0.5 * acc_ref[...]
    if final_norm:
        y = _rms(y, fg_ref[...])
    o_ref[...] = y


def _ffn(x2, g, wg, wu, wd, fg, final_norm):
    n_tok, d = x2.shape
    tm = min(FFN_ROWS, n_tok)
    n_chunks = wg.shape[1] // FFN_COLS
    kern = functools.partial(_ffn_kernel, n_chunks=n_chunks, final_norm=final_norm)
    return pl.pallas_call(
        kern,
        grid=(n_tok // tm,),
        in_specs=[pl.BlockSpec((tm, d), lambda i: (i, 0)),
                  _const_spec(g.shape), _const_spec(wg.shape), _const_spec(wu.shape),
                  _const_spec(wd.shape), _const_spec(fg.shape)],
        out_specs=pl.BlockSpec((tm, d), lambda i: (i, 0)),
        out_shape=jax.ShapeDtypeStruct((n_tok, d), F32),
        scratch_shapes=[pltpu.VMEM((tm, d), F32)],
        compiler_params=_cparams(("parallel",)),
        name="ffn",
    )(x2, g, wg, wu, wd, fg)


def _mixer_ab_kernel(x_ref, ng_ref, wbig_ref, bbig_ref, wgc_ref, bgc_ref, wgr_ref, bgr_ref,
                     ang_ref, bng_ref, lb_ref, wout_ref, c0_ref, n0_ref, m0_ref, s0_ref,
                     xo_ref, c_ref, n_ref, m_ref, s_ref, st_ref, hmix_ref, *, rows, hchunk):
    T = rows
    W = HEADS * HEAD_DIM
    li = pl.program_id(1)

    @pl.when(li == 0)
    def _():
        c_ref[...] = c0_ref[...]
        n_ref[...] = n0_ref[...]
        m_ref[...] = m0_ref[...]
        for h in range(HEADS):
            st_ref[h] = s0_ref[h].T

    x = x_ref[...]
    xn = _rms(x, ng_ref[...]).astype(BF16)

    def proj(k):
        return _dot(xn, wbig_ref[:, k * W:(k + 1) * W]) + bbig_ref[:, k * W:(k + 1) * W]

    aq = proj(0) * (HEAD_DIM ** -0.5)
    ak = proj(1)
    av = proj(2)
    ao = proj(3)
    zc = _dot(xn, wgc_ref[...]) + bgc_ref[...]
    zr = _dot_nt(wgr_ref[...], xn) + bgr_ref[...]
    ri = lax.broadcasted_iota(jnp.int32, (T, T), 0)
    ci = lax.broadcasted_iota(jnp.int32, (T, T), 1)
    causal = ri >= ci
    tril = jnp.where(causal, 1.0, 0.0).astype(BF16)
    triu = jnp.where(ri <= ci, 1.0, 0.0).astype(BF16)
    fc = sum(_dot(tril, p) for p in _split3(_log_sigmoid(zc)))
    fr = sum(_dot(p, triu) for p in _split3(_log_sigmoid(zr)))
    for h in range(HEADS):
        cols = slice(h * HEAD_DIM, (h + 1) * HEAD_DIM)
        ig_c = zc[:, h:h + 1]
        ig_r = zr[h:h + 1, :]
        f_c = fc[:, HEADS + h:HEADS + h + 1]
        f_r = fr[HEADS + h:HEADS + h + 1, :]
        m_prev = m_ref[h:h + 1, 0:1]
        log_src = jnp.where(causal, f_c + (ig_r - f_r), -jnp.inf)
        log_prev = f_c + m_prev
        m_t = jnp.maximum(log_prev, jnp.max(log_src, axis=-1, keepdims=True))
        w_src = jnp.exp(log_src - m_t)
        w_prev = jnp.exp(log_prev - m_t)
        qf = aq[:, cols]
        kf = ak[:, cols]
        q = qf.astype(BF16)
        v = av[:, cols].astype(BF16)
        s = _dot_nt(q, kf.astype(BF16)) * w_src
        c_old = c_ref[h]
        n_old = n_ref[h:h + 1, :]
        num = _dot(s.astype(BF16), v) + w_prev * _dot(q, c_old.astype(BF16))
        den = jnp.sum(s, axis=-1, keepdims=True) + w_prev * jnp.sum(qf * n_old, axis=-1, keepdims=True)
        den = jnp.maximum(jnp.abs(den), jnp.exp(-m_t))
        hh = num / den
        f_last = f_c[T - 1:T, :]
        m_new = m_t[T - 1:T, :]
        decay = jnp.exp(f_last + m_prev - m_new)
        w_end = jnp.exp(f_last - f_c + ig_c - m_new)
        wk = w_end * kf
        c_ref[h] = decay * c_old + _dot_tn(wk.astype(BF16), v)
        n_ref[h:h + 1, :] = decay * n_old + jnp.sum(wk, axis=0, keepdims=True)
        m_ref[h:h + 1, :] = jnp.broadcast_to(m_new, (1, N_GATE_LANES))
        y = hh * lax.rsqrt(jnp.mean(hh * hh, axis=-1, keepdims=True) + EPS)
        out = (y * ang_ref[:, cols]) * _sigmoid(ao[:, cols])
        hmix_ref[:, cols] = out.astype(BF16)

    qh = proj(4)
    qh = qh * _sigmoid(qh)
    zf = proj(5)
    iv = proj(6)
    bg = proj(7)
    lb = lb_ref[...]
    lf = jnp.log(lb + (1.0 - lb) * _sigmoid(zf))
    kk = (1.0 - lb) * _sigmoid(-zf)
    sh = int(math.log2(hchunk))
    same_chunk = lax.shift_right_logical(ri, sh) == lax.shift_right_logical(ci, sh)
    blocktril = jnp.where(causal & same_chunk, 1.0, 0.0).astype(BF16)
    G = sum(_dot(blocktril, p) for p in _split3(lf))
    q_dec = (qh * jnp.exp(G)).astype(BF16)
    k_inv = (kk * jnp.exp(-G)).astype(BF16)
    ivb = iv.astype(BF16)
    rs = lax.broadcasted_iota(jnp.int32, (hchunk, hchunk), 0)
    cs = lax.broadcasted_iota(jnp.int32, (hchunk, hchunk), 1)
    sub_causal = rs >= cs
    for j in range(T // hchunk):
        r = slice(j * hchunk, (j + 1) * hchunk)
        g_end = G[(j + 1) * hchunk - 1:(j + 1) * hchunk, :]
        k_end = (kk[r, :] * jnp.exp(g_end - G[r, :])).astype(BF16)
        e_end = jnp.exp(g_end)
        for h in range(HEADS):
            cols = slice(h * HEAD_DIM, (h + 1) * HEAD_DIM)
            qd = q_dec[r, cols]
            att = jnp.where(sub_causal, _dot_nt(qd, k_inv[r, cols]), 0.0)
            st_old = st_ref[h]
            o = _dot(att.astype(BF16), ivb[r, cols]) + _dot_nt(qd, st_old.astype(BF16))
            st_ref[h] = st_old * e_end[:, cols] + _dot_tn(ivb[r, cols], k_end[:, cols])
            y = o * lax.rsqrt(jnp.mean(o * o, axis=-1, keepdims=True) + EPS)
            gate = bg[r, cols]
            out = (y * bng_ref[:, cols]) * (gate * _sigmoid(gate))
            hmix_ref[r, W + h * HEAD_DIM:W + (h + 1) * HEAD_DIM] = out.astype(BF16)

    xo_ref[...] = x + _dot(hmix_ref[...], wout_ref[...])

    @pl.when(li == pl.num_programs(1) - 1)
    def _():
        for h in range(HEADS):
            s_ref[h] = st_ref[h].T


def _mixer_ab(x, ng, w, c0, n0, m0, s0):
    B, L, D = x.shape
    T = min(MIX_ROWS, L)
    hchunk = HGRN_CHUNK if T % HGRN_CHUNK == 0 else T
    W = HEADS * HEAD_DIM
    kern = functools.partial(_mixer_ab_kernel, rows=T, hchunk=hchunk)
    state4 = pl.BlockSpec((None, HEADS, HEAD_DIM, HEAD_DIM), lambda b, l: (b, 0, 0, 0))
    state_n = pl.BlockSpec((None, HEADS, HEAD_DIM), lambda b, l: (b, 0, 0))
    state_m = pl.BlockSpec((None, 8, N_GATE_LANES), lambda b, l: (b, 0, 0))
    xspec = pl.BlockSpec((None, T, D), lambda b, l: (b, l, 0))
    consts = (ng, w["w_big"], w["b_big"], w["w_gc"], w["b_gc"], w["w_gr"], w["b_gr"],
              w["a_norm_g"], w["b_norm_g"], w["lb"], w["w_out"])
    return pl.pallas_call(
        kern,
        grid=(B, L // T),
        in_specs=[xspec] + [_const_spec(c.shape) for c in consts] + [state4, state_n, state_m, state4],
        out_specs=[xspec, state4, state_n, state_m, state4],
        out_shape=[jax.ShapeDtypeStruct((B, L, D), F32),
                   jax.ShapeDtypeStruct((B, HEADS, HEAD_DIM, HEAD_DIM), F32),
                   jax.ShapeDtypeStruct((B, HEADS, HEAD_DIM), F32),
                   jax.ShapeDtypeStruct((B, 8, N_GATE_LANES), F32),
                   jax.ShapeDtypeStruct((B, HEADS, HEAD_DIM, HEAD_DIM), F32)],
        scratch_shapes=[pltpu.VMEM((HEADS, HEAD_DIM, HEAD_DIM), F32),
                        pltpu.VMEM((T, 2 * W), BF16)],
        compiler_params=_cparams(("parallel", "arbitrary")),
        name="mixer_ab",
    )(x, *consts, c0, n0, m0, s0)


N_CD_CONSTS = 24


def _mixer_cd_kernel(x_ref, *refs, n_seq, **dims):
    consts = refs[:N_CD_CONSTS]
    per_seq = refs[N_CD_CONSTS:]
    h0_ref, conv0_ref, sre0_ref, sim0_ref, _, h_ref, conv_ref, sre_ref, sim_ref = per_seq[:9]

    @pl.when(pl.program_id(1) == 0)
    def _():
        h_ref[...] = h0_ref[...]
        conv_ref[...] = conv0_ref[...]
        sre_ref[...] = sre0_ref[...]
        sim_ref[...] = sim0_ref[...]

    _run_staggered([_mixer_cd_one(x_ref.at[s], *consts, *[r.at[s] for r in per_seq], **dims)
                    for s in range(n_seq)], CD_PHASE_PIECES)


def _mixer_cd_one(x_ref, ng_ref, perm_ref, permt_ref, wcd_ref, bcd_ref, cw_ref, cb_ref, wa_ref, ba_ref,
                  wx_ref, bx_ref, lam_ref, wbu_ref, abr_ref, abi_ref, pr_ref, pi_ref, pwr_ref, pwi_ref,
                  wc_ref, dsk_ref, wglu_ref, bglu_ref, wout_ref, h0_ref, conv0_ref, sre0_ref, sim0_ref,
                  xo_ref, h_ref, conv_ref, sre_ref, sim_ref, ext_ref, lre_ref, lim_ref, sb_ref, hmix_ref,
                  *, rows, c_width, d_width, n_state, gate_tile):
    T = rows
    n = T // 8
    log_n = int(math.log2(n))
    halo = (CONV_W - 1) * 8
    x = x_ref[...]
    xn = _rms(x, ng_ref[...]).astype(BF16)
    xp = _dot(perm_ref[...], xn).astype(BF16)
    yield

    z_tiles = []
    for k in range((2 * c_width + d_width) // gate_tile):
        cols = slice(k * gate_tile, (k + 1) * gate_tile)
        z_tiles.append(_dot(xp, wcd_ref[:, cols]) + bcd_ref[:, cols])
        yield
    z = jnp.concatenate(z_tiles, axis=1)
    cg = z[:, :c_width]
    cr = z[:, c_width:2 * c_width]
    du = z[:, 2 * c_width:]

    hist = conv_ref[...]
    ext_ref[halo:halo + T, :] = cr
    sub_c = lax.broadcasted_iota(jnp.int32, (8, c_width), 0)
    for k in range(CONV_W - 1):
        g = n - (CONV_W - 1) + k
        grp = cr[g * 8:(g + 1) * 8, :]
        ext_ref[k * 8:(k + 1) * 8, :] = jnp.where(sub_c == 0, hist[k:k + 1, :], pltpu.roll(grp, 1, 0))
        conv_ref[k:k + 1, :] = grp[7:8, :]
    u = cb_ref[...] + cw_ref[CONV_W - 1:CONV_W, :] * cr
    for k in range(CONV_W - 1):
        u = u + cw_ref[k:k + 1, :] * ext_ref[k * 8:k * 8 + T, :]
    yield

    ub = u.astype(BF16)
    log_sig_lam = _log_sigmoid(lam_ref[...])
    sub_t = lax.broadcasted_iota(jnp.int32, (8, gate_tile), 0)
    for k in range(c_width // gate_tile):
        cols = slice(k * gate_tile, (k + 1) * gate_tile)
        r = _sigmoid(_dot(ub[:, cols], wa_ref[k]) + ba_ref[:, cols])
        i = _sigmoid(_dot(ub[:, cols], wx_ref[k]) + bx_ref[:, cols])
        log_a = (RG_C * r) * log_sig_lam[:, cols]
        a = jnp.exp(log_a)
        b = jnp.sqrt(-jnp.tanh(log_a) * (a * a + 1.0)) * (i * u[:, cols])
        dec = a[0:8, :]
        loc = b[0:8, :]
        decs, locs = [dec], [loc]
        for j in range(1, n):
            aj = a[j * 8:(j + 1) * 8, :]
            loc = aj * loc + b[j * 8:(j + 1) * 8, :]
            dec = aj * dec
            decs.append(dec)
            locs.append(loc)
        h_prev = h_ref[:, cols]
        ea = dec
        eb = loc + jnp.where(sub_t == 0, dec * h_prev, 0.0)
        for d in (1, 2, 4):
            a_sh = jnp.where(sub_t >= d, pltpu.roll(ea, d, 0), 1.0)
            b_sh = jnp.where(sub_t >= d, pltpu.roll(eb, d, 0), 0.0)
            eb = ea * b_sh + eb
            ea = ea * a_sh
        carry = jnp.where(sub_t == 0, h_prev, pltpu.roll(eb, 1, 0))
        h_ref[:, cols] = eb[7:8, :]
        h = jnp.concatenate([locs[j] + decs[j] * carry for j in range(n)], axis=0)
        hmix_ref[:, cols] = (jax.nn.gelu(cg[:, cols], approximate=True) * h).astype(BF16)
        yield

    bu = _dot(du.astype(BF16), wbu_ref[...])
    p_re = jnp.broadcast_to(abr_ref[...], (8, n_state))
    p_im = jnp.broadcast_to(abi_ref[...], (8, n_state))
    l_re = bu[0:8, :n_state]
    l_im = bu[0:8, n_state:]
    lre_ref[0:8, :] = l_re
    lim_ref[0:8, :] = l_im
    for j in range(1, n):
        rows_j = slice(j * 8, (j + 1) * 8)
        l_re, l_im = (bu[rows_j, :n_state] + (p_re * l_re - p_im * l_im),
                      bu[rows_j, n_state:] + (p_re * l_im + p_im * l_re))
        lre_ref[rows_j, :] = l_re
        lim_ref[rows_j, :] = l_im
        if j == n // 2 or j == n - 1:
            yield
    sub_s = lax.broadcasted_iota(jnp.int32, (8, n_state), 0)
    s_re = sre_ref[...]
    s_im = sim_ref[...]
    q_re = pr_ref[log_n:log_n + 1, :]
    q_im = pi_ref[log_n:log_n + 1, :]
    e_re = l_re + jnp.where(sub_s == 0, q_re * s_re - q_im * s_im, 0.0)
    e_im = l_im + jnp.where(sub_s == 0, q_re * s_im + q_im * s_re, 0.0)
    for k, d in enumerate((1, 2, 4)):
        q_re = pr_ref[log_n + k:log_n + k + 1, :]
        q_im = pi_ref[log_n + k:log_n + k + 1, :]
        r_sh = jnp.where(sub_s >= d, pltpu.roll(e_re, d, 0), 0.0)
        i_sh = jnp.where(sub_s >= d, pltpu.roll(e_im, d, 0), 0.0)
        e_re, e_im = e_re + (q_re * r_sh - q_im * i_sh), e_im + (q_re * i_sh + q_im * r_sh)
    c_re = jnp.where(sub_s == 0, s_re, pltpu.roll(e_re, 1, 0))
    c_im = jnp.where(sub_s == 0, s_im, pltpu.roll(e_im, 1, 0))
    sre_ref[...] = e_re[7:8, :]
    sim_ref[...] = e_im[7:8, :]
    for j in range(0, n, 2):
        f_re, f_im = [], []
        for jj in (j, j + 1):
            rows_j = slice(jj * 8, (jj + 1) * 8)
            w_re = pwr_ref[rows_j, :]
            w_im = pwi_ref[rows_j, :]
            f_re.append(lre_ref[rows_j, :] + (w_re * c_re - w_im * c_im))
            f_im.append(lim_ref[rows_j, :] + (w_re * c_im + w_im * c_re))
        sb_ref[j * 8:(j + 2) * 8, :n_state] = jnp.concatenate(f_re, axis=0).astype(BF16)
        sb_ref[j * 8:(j + 2) * 8, n_state:] = jnp.concatenate(f_im, axis=0).astype(BF16)
    yield
    y = _dot(sb_ref[...], wc_ref[...]) + dsk_ref[...] * du
    y = jax.nn.gelu(y, approximate=True)
    y = y * _sigmoid(_dot(y.astype(BF16), wglu_ref[...]) + bglu_ref[...])
    hmix_ref[:, c_width:c_width + d_width] = y.astype(BF16)
    yield

    d_model = x.shape[-1]
    mixed = []
    for k in range(d_model // gate_tile):
        cols = slice(k * gate_tile, (k + 1) * gate_tile)
        mixed.append(_dot(permt_ref[...], hmix_ref[:, cols]).astype(BF16))
        yield
    mixed = jnp.concatenate(mixed, axis=1)
    for k in range(d_model // gate_tile):
        cols = slice(k * gate_tile, (k + 1) * gate_tile)
        xo_ref[:, cols] = x[:, cols] + _dot(mixed, wout_ref[:, cols])
        yield


def _mixer_cd(x, ng, w, h0, conv0, sre0, sim0):
    B, L, D = x.shape
    T = min(MIX_ROWS, L)
    c_width = h0.shape[-1]
    n_state = sre0.shape[-1]
    d_width = w["w_glu"].shape[0]
    nb = MIX_SEQS if B % MIX_SEQS == 0 else 1
    kern = functools.partial(_mixer_cd_kernel, n_seq=nb, rows=T, c_width=c_width, d_width=d_width,
                             n_state=n_state, gate_tile=w["w_a"].shape[-1])
    xspec = pl.BlockSpec((nb, T, D), lambda b, l: (b, l, 0))
    st_h = pl.BlockSpec((nb, 1, c_width), lambda b, l: (b, 0, 0))
    st_conv = pl.BlockSpec((nb, CONV_W - 1, c_width), lambda b, l: (b, 0, 0))
    st_s = pl.BlockSpec((nb, 1, n_state), lambda b, l: (b, 0, 0))
    perm, perm_t = _regroup_matrices(T)
    n_pow = (T // 8) * 8
    consts = (ng, perm, perm_t, w["w_cd"], w["b_cd"], w["conv_w"], w["conv_b"], w["w_a"], w["b_a"], w["w_x"],
              w["b_x"], w["lam"], w["w_bu"], w["ab_re"], w["ab_im"], w["pow_re"], w["pow_im"],
              w["run_re"][:n_pow], w["run_im"][:n_pow], w["w_c"], w["d_skip"], w["w_glu"], w["b_glu"], w["w_out"])
    assert len(consts) == N_CD_CONSTS
    return pl.pallas_call(
        kern,
        grid=(B // nb, L // T),
        in_specs=[xspec] + [_const_spec(c.shape) for c in consts] + [st_h, st_conv, st_s, st_s],
        out_specs=[xspec, st_h, st_conv, st_s, st_s],
        out_shape=[jax.ShapeDtypeStruct((B, L, D), F32),
                   jax.ShapeDtypeStruct((B, 1, c_width), F32),
                   jax.ShapeDtypeStruct((B, CONV_W - 1, c_width), F32),
                   jax.ShapeDtypeStruct((B, 1, n_state), F32),
                   jax.ShapeDtypeStruct((B, 1, n_state), F32)],
        scratch_shapes=[pltpu.VMEM((nb, T + (CONV_W - 1) * 8, c_width), F32),
                        pltpu.VMEM((nb, T, n_state), F32),
                        pltpu.VMEM((nb, T, n_state), F32),
                        pltpu.VMEM((nb, T, 2 * n_state), BF16),
                        pltpu.VMEM((nb, T, c_width + d_width), BF16)],
        compiler_params=_cparams(("parallel", "arbitrary")),
        name="mixer_cd",
    )(x, *consts, h0, conv0, sre0, sim0)


def _regroup_matrices(rows):
    n = rows // 8
    r = np.arange(rows)
    p = np.zeros((rows, rows), np.float32)
    p[r, (r % 8) * n + r // 8] = 1.0
    return jnp.asarray(p, BF16), jnp.asarray(p.T, BF16)


def _prep_s5_kernel(are_ref, aim_ref, ldt_ref, wbre_ref, wbim_ref, abr_ref, abi_ref, pr_ref, pi_ref,
                    runr_ref, runi_ref, wbu_ref, *, n_state):
    a_re = are_ref[...]
    a_im = aim_ref[...]
    dt = jnp.exp(ldt_ref[...])
    mag = jnp.exp(dt * a_re)
    ab_re = mag * jnp.cos(dt * a_im)
    ab_im = mag * jnp.sin(dt * a_im)
    inv = 1.0 / (a_re * a_re + a_im * a_im)
    z_re = ((ab_re - 1.0) * a_re + ab_im * a_im) * inv
    z_im = (ab_im * a_re - (ab_re - 1.0) * a_im) * inv
    abr_ref[...] = ab_re
    abi_ref[...] = ab_im
    wb_re = wbre_ref[...]
    wb_im = wbim_ref[...]
    wbu_ref[:, :n_state] = (z_re * wb_re - z_im * wb_im).astype(BF16)
    wbu_ref[:, n_state:] = (z_re * wb_im + z_im * wb_re).astype(BF16)
    p_re, p_im = ab_re, ab_im
    for k in range(MAX_SCAN_STEPS):
        pr_ref[k:k + 1, :] = p_re
        pi_ref[k:k + 1, :] = p_im
        p_re, p_im = p_re * p_re - p_im * p_im, 2.0 * (p_re * p_im)
    p_re, p_im = ab_re, ab_im
    for j in range(MIX_ROWS // 8):
        runr_ref[j * 8:(j + 1) * 8, :] = jnp.broadcast_to(p_re, (8, n_state))
        runi_ref[j * 8:(j + 1) * 8, :] = jnp.broadcast_to(p_im, (8, n_state))
        p_re, p_im = p_re * ab_re - p_im * ab_im, p_re * ab_im + p_im * ab_re


def _prep_s5(a_re, a_im, log_dt_rep, wb_re, wb_im):
    n_state = a_re.shape[-1]
    d_width = wb_re.shape[0]
    kern = functools.partial(_prep_s5_kernel, n_state=n_state)
    row = jax.ShapeDtypeStruct((1, n_state), F32)
    tab = jax.ShapeDtypeStruct((MAX_SCAN_STEPS, n_state), F32)
    run = jax.ShapeDtypeStruct((MIX_ROWS, n_state), F32)
    return pl.pallas_call(
        kern,
        out_shape=[row, row, tab, tab, run, run, jax.ShapeDtypeStruct((d_width, 2 * n_state), BF16)],
        name="prep_s5",
    )(a_re, a_im, log_dt_rep, wb_re, wb_im)


def _prep_lb_kernel(logits_ref, lb_ref):
    lg = logits_ref[...]
    n = lg.shape[0]
    mx = lg[0:1, :]
    for i in range(1, n):
        mx = jnp.maximum(mx, lg[i:i + 1, :])
    e = [jnp.exp(lg[i:i + 1, :] - mx) for i in range(n)]
    tot = e[0]
    for i in range(1, n):
        tot = tot + e[i]
    run = None
    for i in range(n):
        p = e[i] / tot
        run = p if run is None else run + p
        lb_ref[i:i + 1, :] = run


def _prep_lb(logits):
    return pl.pallas_call(_prep_lb_kernel, out_shape=jax.ShapeDtypeStruct(logits.shape, F32),
                          name="prep_lb")(logits)


def _block_diag(blocks):
    n, r, c = blocks.shape
    eye = jnp.eye(n, dtype=blocks.dtype)
    return (eye[:, None, :, None] * blocks[:, :, None, :]).reshape(n * r, n * c)


def kernel(x_prompt, x_sample, state_mlstm_C, state_mlstm_n, state_mlstm_m, state_hgrn_S, state_rglru_h,
           state_rglru_conv, state_s5_re, state_s5_im, norm_g, final_norm_g, ffn_w_gate, ffn_w_up, ffn_w_down,
           ab_w_in, ab_b_in, mlstm_norm_g, hgrn_norm_g, hgrn_lb_logits, ab_w_out, cd_w_in, cd_b_in, conv_w,
           conv_b, rg_w_a, rg_b_a, rg_w_x, rg_b_x, rg_lambda, s5_A_re, s5_A_im, s5_log_dt, s5_B_re, s5_B_im,
           s5_C_re, s5_C_im, s5_D, s5_w_glu, s5_b_glu, cd_w_out):
    depth = norm_g.shape[0]
    d_model = x_prompt.shape[-1]
    W = HEADS * HEAD_DIM
    c_width = rg_lambda.shape[-1]
    groups, n_st, s5_ch = s5_B_re.shape[1:]
    n_state = groups * n_st
    d_width = groups * s5_ch
    gate_tile = 256
    blocks_per_tile = gate_tile // rg_w_a.shape[-1]

    lb_all = _prep_lb(hgrn_lb_logits.astype(F32))

    ffn = []
    for l in range(depth):
        ffn.append([(ffn_w_gate[l, i].astype(BF16), ffn_w_up[l, i].astype(BF16), ffn_w_down[l, i].astype(BF16))
                    for i in range(2)])
    mix_w = []
    for l in range(depth):
        j = l // 2
        if l % 2 == 0:
            wi, bi = ab_w_in[j], ab_b_in[j]
            big = jnp.concatenate([wi[:, :4 * W], wi[:, 4 * W + 2 * HEADS:]], axis=1)
            b_big = jnp.concatenate([bi[:4 * W], bi[4 * W + 2 * HEADS:]])[None, :]
            wg = wi[:, 4 * W:4 * W + 2 * HEADS]
            bgate = bi[4 * W:4 * W + 2 * HEADS]
            mix_w.append(dict(
                w_big=big.astype(BF16), b_big=b_big,
                w_gc=jnp.pad(wg, ((0, 0), (0, N_GATE_LANES - 2 * HEADS))).astype(BF16),
                b_gc=jnp.pad(bgate, (0, N_GATE_LANES - 2 * HEADS))[None, :],
                w_gr=jnp.pad(wg.T, ((0, N_GATE_ROWS - 2 * HEADS), (0, 0))).astype(BF16),
                b_gr=jnp.pad(bgate, (0, N_GATE_ROWS - 2 * HEADS))[:, None],
                a_norm_g=mlstm_norm_g[j][None, :], b_norm_g=hgrn_norm_g[j][None, :],
                lb=lb_all[l][None, :], w_out=ab_w_out[j].astype(BF16)))
        else:
            n_tiles = c_width // gate_tile

            def tiles(wb):
                wb = wb.reshape(n_tiles, blocks_per_tile, wb.shape[-2], wb.shape[-1])
                return jnp.stack([_block_diag(wb[t]) for t in range(n_tiles)]).astype(BF16)

            wb_re = _block_diag(jnp.swapaxes(s5_B_re[j], 1, 2))
            wb_im = _block_diag(jnp.swapaxes(s5_B_im[j], 1, 2))
            ab_re, ab_im, pow_re, pow_im, run_re, run_im, w_bu = _prep_s5(
                s5_A_re[j].reshape(1, n_state), s5_A_im[j].reshape(1, n_state),
                jnp.repeat(s5_log_dt[j], n_st)[None, :], wb_re, wb_im)
            w_c = jnp.concatenate([_block_diag(jnp.swapaxes(s5_C_re[j], 1, 2)),
                                   -_block_diag(jnp.swapaxes(s5_C_im[j], 1, 2))], axis=0)
            mix_w.append(dict(
                w_cd=cd_w_in[j].astype(BF16), b_cd=cd_b_in[j][None, :], conv_w=conv_w[j], conv_b=conv_b[j][None, :],
                w_a=tiles(rg_w_a[j]), b_a=rg_b_a[j][None, :], w_x=tiles(rg_w_x[j]), b_x=rg_b_x[j][None, :],
                lam=rg_lambda[j][None, :], w_bu=w_bu, ab_re=ab_re, ab_im=ab_im, pow_re=pow_re, pow_im=pow_im,
                run_re=run_re, run_im=run_im,
                w_c=w_c.astype(BF16), d_skip=s5_D[j][None, :], w_glu=s5_w_glu[j].astype(BF16),
                b_glu=s5_b_glu[j][None, :], w_out=cd_w_out[j].astype(BF16)))
    fg = final_norm_g[None, :]

    def run_trunk(x, states):
        mC, mn, mm, hS, rh, rc, sre, sim = states
        B, L, _ = x.shape
        outs = ([], [], [], [], [], [], [], [])
        for l in range(depth):
            j = l // 2
            wg, wu, wd = ffn[l][0]
            x = _ffn(x.reshape(B * L, d_model), norm_g[l, 0][None, :], wg, wu, wd, fg, False).reshape(B, L, d_model)
            if l % 2 == 0:
                m0 = jnp.broadcast_to(jnp.pad(mm[j], ((0, 0), (0, 8 - HEADS)))[:, :, None], (B, 8, N_GATE_LANES))
                x, c1, n1, m1, s1 = _mixer_ab(x, norm_g[l, 1][None, :], mix_w[l], mC[j], mn[j], m0, hS[j])
                for lst, val in zip(outs[:4], (c1, n1, m1[:, :HEADS, 0], s1)):
                    lst.append(val)
            else:
                x, h1, buf1, re1, im1 = _mixer_cd(x, norm_g[l, 1][None, :], mix_w[l], rh[j][:, None, :], rc[j],
                                                  sre[j].reshape(B, 1, n_state), sim[j].reshape(B, 1, n_state))
                for lst, val in zip(outs[4:], (h1[:, 0], buf1, re1.reshape(B, groups, n_st),
                                               im1.reshape(B, groups, n_st))):
                    lst.append(val)
            wg, wu, wd = ffn[l][1]
            x = _ffn(x.reshape(B * L, d_model), norm_g[l, 2][None, :], wg, wu, wd, fg,
                     l == depth - 1).reshape(B, L, d_model)
        return x, [jnp.stack(lst) for lst in outs]

    n_even = (depth + 1) // 2
    n_odd = depth // 2
    bp = x_prompt.shape[0]
    zero_states = (jnp.zeros((n_even, bp, HEADS, HEAD_DIM, HEAD_DIM), F32),
                   jnp.zeros((n_even, bp, HEADS, HEAD_DIM), F32),
                   jnp.zeros((n_even, bp, HEADS), F32),
                   jnp.zeros((n_even, bp, HEADS, HEAD_DIM, HEAD_DIM), F32),
                   jnp.zeros((n_odd, bp, c_width), F32),
                   jnp.zeros((n_odd, bp, CONV_W - 1, c_width), F32),
                   jnp.zeros((n_odd, bp, groups, n_st), F32),
                   jnp.zeros((n_odd, bp, groups, n_st), F32))
    y_prompt, p_st = run_trunk(x_prompt, zero_states)
    y_sample, s_st = run_trunk(x_sample, (state_mlstm_C, state_mlstm_n, state_mlstm_m, state_hgrn_S,
                                          state_rglru_h, state_rglru_conv, state_s5_re, state_s5_im))
    return (y_prompt, y_sample, *p_st, *s_st)
```

```python
import functools
import math

import jax
import jax.numpy as jnp
import numpy as np
from jax import lax
from jax.experimental import pallas as pl
from jax.experimental.pallas import tpu as pltpu

F32 = jnp.float32
BF16 = jnp.bfloat16

EPS = 1e-6
RG_C = 8.0
CONV_W = 4
HEADS = 4
HEAD_DIM = 128
N_GATE_LANES = 128
N_GATE_ROWS = 16
V7X_VMEM_LIMIT = 56 * 1024 * 1024
FFN_COLS = 256
FFN_ROWS = 512
MIX_ROWS = 256
HGRN_CHUNK = 64
MAX_SCAN_STEPS = 8
AB_SEQS = 4
AB_STAGGER = 9
OUT_COLS = 256
MIX_SEQS = 4
CD_PHASE_PIECES = 8


def _dot(a, b):
    return jnp.dot(a, b, preferred_element_type=F32)


def _dot_nt(a, b):
    return lax.dot_general(a, b, (((1,), (1,)), ((), ())), preferred_element_type=F32)


def _dot_tn(a, b):
    return lax.dot_general(a, b, (((0,), (0,)), ((), ())), preferred_element_type=F32)


def _rms(x, g):
    y = x * lax.rsqrt(jnp.mean(x * x, axis=-1, keepdims=True) + EPS)
    return y * g


def _sigmoid(x):
    return jax.nn.sigmoid(x)


def _log_sigmoid(x):
    return jnp.minimum(x, 0.0) - jnp.log1p(jnp.exp(-jnp.abs(x)))


def _split3(a):
    hi = a.astype(BF16)
    r = a - hi.astype(F32)
    mid = r.astype(BF16)
    lo = (r - mid.astype(F32)).astype(BF16)
    return hi, mid, lo


def _const_spec(shape):
    nd = len(shape)
    return pl.BlockSpec(shape, lambda *_: (0,) * nd, pipeline_mode=pl.Buffered(1))


def _run_staggered(piece_generators, offset):
    pending = list(piece_generators)
    live = []
    tick = 0
    while pending or live:
        if pending and tick % offset == 0:
            live.append(pending.pop(0))
        for g in list(live):
            try:
                next(g)
            except StopIteration:
                live.remove(g)
        tick += 1


def _cparams(sem):
    return pltpu.CompilerParams(dimension_semantics=sem, vmem_limit_bytes=V7X_VMEM_LIMIT)


def _ffn_kernel(x_ref, g_ref, wg_ref, wu_ref, wd_ref, fg_ref, o_ref, acc_ref, *, n_chunks, final_norm):
    x = x_ref[...]
    xn = _rms(x, g_ref[...]).astype(BF16)
    for j in range(n_chunks):
        cols = slice(j * FFN_COLS, (j + 1) * FFN_COLS)
        gate = _dot(xn, wg_ref[:, cols])
        up = _dot(xn, wu_ref[:, cols])
        h = ((gate * _sigmoid(gate)) * up).astype(BF16)
        d = _dot(h, wd_ref[cols, :])
        if j == 0:
            acc_ref[...] = d
        else:
            acc_ref[...] += d
    y = x + 0.5 * acc_ref[...]
    if final_norm:
        y = _rms(y, fg_ref[...])
    o_ref[...] = y


def _ffn(x2, g, wg, wu, wd, fg, final_norm):
    n_tok, d = x2.shape
    tm = min(FFN_ROWS, n_tok)
    n_chunks = wg.shape[1] // FFN_COLS
    kern = functools.partial(_ffn_kernel, n_chunks=n_chunks, final_norm=final_norm)
    return pl.pallas_call(
        kern,
        grid=(n_tok // tm,),
        in_specs=[pl.BlockSpec((tm, d), lambda i: (i, 0)),
                  _const_spec(g.shape), _const_spec(wg.shape), _const_spec(wu.shape),
                  _const_spec(wd.shape), _const_spec(fg.shape)],
        out_specs=pl.BlockSpec((tm, d), lambda i: (i, 0)),
        out_shape=jax.ShapeDtypeStruct((n_tok, d), F32),
        scratch_shapes=[pltpu.VMEM((tm, d), F32)],
        compiler_params=_cparams(("parallel",)),
        name="ffn",
    )(x2, g, wg, wu, wd, fg)


N_AB_CONSTS = 11


def _mixer_ab_kernel(x_ref, *refs, n_seq, **dims):
    consts = refs[:N_AB_CONSTS]
    per_seq = refs[N_AB_CONSTS:]
    c0_ref, n0_ref, m0_ref, s0_ref, _, c_ref, n_ref, m_ref, s_ref, st_ref = per_seq[:10]
    li = pl.program_id(1)

    @pl.when(li == 0)
    def _():
        c_ref[...] = c0_ref[...]
        n_ref[...] = n0_ref[...]
        m_ref[...] = m0_ref[...]
        for s in range(n_seq):
            for h in range(HEADS):
                st_ref[s, h] = s0_ref[s, h].T

    _run_staggered([_mixer_ab_one(x_ref.at[s], *consts, *[r.at[s] for r in per_seq], **dims)
                    for s in range(n_seq)], AB_STAGGER)

    @pl.when(li == pl.num_programs(1) - 1)
    def _():
        for s in range(n_seq):
            for h in range(HEADS):
                s_ref[s, h] = st_ref[s, h].T


def _mixer_ab_one(x_ref, ng_ref, wbig_ref, bbig_ref, wgc_ref, bgc_ref, wgr_ref, bgr_ref,
                  ang_ref, bng_ref, lb_ref, wout_ref, c0_ref, n0_ref, m0_ref, s0_ref,
                  xo_ref, c_ref, n_ref, m_ref, s_ref, st_ref, hmix_ref, *, rows, hchunk):
    T = rows
    W = HEADS * HEAD_DIM
    x = x_ref[...]
    xn = _rms(x, ng_ref[...]).astype(BF16)
    zc = _dot(xn, wgc_ref[...]) + bgc_ref[...]
    zr = _dot_nt(wgr_ref[...], xn) + bgr_ref[...]
    yield

    z = []
    for k in range(8):
        halves = []
        for c0 in (k * W, k * W + W // 2):
            halves.append(_dot(xn, wbig_ref[:, c0:c0 + W // 2]) + bbig_ref[:, c0:c0 + W // 2])
            yield
        z.append(jnp.concatenate(halves, axis=1))

    aq = z[0] * (HEAD_DIM ** -0.5)
    ak, av, ao = z[1], z[2], z[3]
    ri = lax.broadcasted_iota(jnp.int32, (T, T), 0)
    ci = lax.broadcasted_iota(jnp.int32, (T, T), 1)
    causal = ri >= ci
    tril = jnp.where(causal, 1.0, 0.0).astype(BF16)
    triu = jnp.where(ri <= ci, 1.0, 0.0).astype(BF16)
    fc = sum(_dot(tril, p) for p in _split3(_log_sigmoid(zc)))
    fr = sum(_dot(p, triu) for p in _split3(_log_sigmoid(zr)))
    yield
    for h in range(HEADS):
        cols = slice(h * HEAD_DIM, (h + 1) * HEAD_DIM)
        ig_c = zc[:, h:h + 1]
        ig_r = zr[h:h + 1, :]
        f_c = fc[:, HEADS + h:HEADS + h + 1]
        f_r = fr[HEADS + h:HEADS + h + 1, :]
        m_prev = m_ref[h:h + 1, 0:1]
        log_src = jnp.where(causal, f_c + (ig_r - f_r), -jnp.inf)
        log_prev = f_c + m_prev
        m_t = jnp.maximum(log_prev, jnp.max(log_src, axis=-1, keepdims=True))
        w_src = jnp.exp(log_src - m_t)
        w_prev = jnp.exp(log_prev - m_t)
        qf = aq[:, cols]
        kf = ak[:, cols]
        q = qf.astype(BF16)
        v = av[:, cols].astype(BF16)
        s = _dot_nt(q, kf.astype(BF16)) * w_src
        yield
        c_old = c_ref[h]
        n_old = n_ref[h:h + 1, :]
        num = _dot(s.astype(BF16), v) + w_prev * _dot(q, c_old.astype(BF16))
        den = jnp.sum(s, axis=-1, keepdims=True) + w_prev * jnp.sum(qf * n_old, axis=-1, keepdims=True)
        den = jnp.maximum(jnp.abs(den), jnp.exp(-m_t))
        hh = num / den
        f_last = f_c[T - 1:T, :]
        m_new = m_t[T - 1:T, :]
        decay = jnp.exp(f_last + m_prev - m_new)
        w_end = jnp.exp(f_last - f_c + ig_c - m_new)
        wk = w_end * kf
        c_ref[h] = decay * c_old + _dot_tn(wk.astype(BF16), v)
        n_ref[h:h + 1, :] = decay * n_old + jnp.sum(wk, axis=0, keepdims=True)
        m_ref[h:h + 1, :] = jnp.broadcast_to(m_new, (1, N_GATE_LANES))
        y = hh * lax.rsqrt(jnp.mean(hh * hh, axis=-1, keepdims=True) + EPS)
        out = (y * ang_ref[:, cols]) * _sigmoid(ao[:, cols])
        hmix_ref[:, cols] = out.astype(BF16)
        yield

    qh = z[4] * _sigmoid(z[4])
    zf, iv, bg = z[5], z[6], z[7]
    lb = lb_ref[...]
    lf = jnp.log(lb + (1.0 - lb) * _sigmoid(zf))
    kk = (1.0 - lb) * _sigmoid(-zf)
    sh = int(math.log2(hchunk))
    same_chunk = lax.shift_right_logical(ri, sh) == lax.shift_right_logical(ci, sh)
    blocktril = jnp.where(causal & same_chunk, 1.0, 0.0).astype(BF16)
    G = sum(_dot(blocktril, p) for p in _split3(lf))
    q_dec = (qh * jnp.exp(G)).astype(BF16)
    k_inv = (kk * jnp.exp(-G)).astype(BF16)
    ivb = iv.astype(BF16)
    yield
    rs = lax.broadcasted_iota(jnp.int32, (hchunk, hchunk), 0)
    cs = lax.broadcasted_iota(jnp.int32, (hchunk, hchunk), 1)
    sub_causal = rs >= cs
    for j in range(T // hchunk):
        r = slice(j * hchunk, (j + 1) * hchunk)
        g_end = G[(j + 1) * hchunk - 1:(j + 1) * hchunk, :]
        k_end = (kk[r, :] * jnp.exp(g_end - G[r, :])).astype(BF16)
        e_end = jnp.exp(g_end)
        for h in range(HEADS):
            cols = slice(h * HEAD_DIM, (h + 1) * HEAD_DIM)
            qd = q_dec[r, cols]
            att = jnp.where(sub_causal, _dot_nt(qd, k_inv[r, cols]), 0.0)
            st_old = st_ref[h]
            o = _dot(att.astype(BF16), ivb[r, cols]) + _dot_nt(qd, st_old.astype(BF16))
            st_ref[h] = st_old * e_end[:, cols] + _dot_tn(ivb[r, cols], k_end[:, cols])
            y = o * lax.rsqrt(jnp.mean(o * o, axis=-1, keepdims=True) + EPS)
            gate = bg[r, cols]
            out = (y * bng_ref[:, cols]) * (gate * _sigmoid(gate))
            hmix_ref[r, W + h * HEAD_DIM:W + (h + 1) * HEAD_DIM] = out.astype(BF16)
            if h % 2 == 1:
                yield

    d_model = x.shape[-1]
    for k in range(d_model // OUT_COLS):
        cols = slice(k * OUT_COLS, (k + 1) * OUT_COLS)
        xo_ref[:, cols] = x[:, cols] + _dot(hmix_ref[...], wout_ref[:, cols])
        yield


def _mixer_ab(x, ng, w, c0, n0, m0, s0):
    B, L, D = x.shape
    T = min(MIX_ROWS, L)
    hchunk = HGRN_CHUNK if T % HGRN_CHUNK == 0 else T
    W = HEADS * HEAD_DIM
    nb = AB_SEQS if B % AB_SEQS == 0 else 1
    kern = functools.partial(_mixer_ab_kernel, n_seq=nb, rows=T, hchunk=hchunk)
    state4 = pl.BlockSpec((nb, HEADS, HEAD_DIM, HEAD_DIM), lambda b, l: (b, 0, 0, 0))
    state_n = pl.BlockSpec((nb, HEADS, HEAD_DIM), lambda b, l: (b, 0, 0))
    state_m = pl.BlockSpec((nb, 8, N_GATE_LANES), lambda b, l: (b, 0, 0))
    xspec = pl.BlockSpec((nb, T, D), lambda b, l: (b, l, 0))
    consts = (ng, w["w_big"], w["b_big"], w["w_gc"], w["b_gc"], w["w_gr"], w["b_gr"],
              w["a_norm_g"], w["b_norm_g"], w["lb"], w["w_out"])
    assert len(consts) == N_AB_CONSTS
    return pl.pallas_call(
        kern,
        grid=(B // nb, L // T),
        in_specs=[xspec] + [_const_spec(c.shape) for c in consts] + [state4, state_n, state_m, state4],
        out_specs=[xspec, state4, state_n, state_m, state4],
        out_shape=[jax.ShapeDtypeStruct((B, L, D), F32),
                   jax.ShapeDtypeStruct((B, HEADS, HEAD_DIM, HEAD_DIM), F32),
                   jax.ShapeDtypeStruct((B, HEADS, HEAD_DIM), F32),
                   jax.ShapeDtypeStruct((B, 8, N_GATE_LANES), F32),
                   jax.ShapeDtypeStruct((B, HEADS, HEAD_DIM, HEAD_DIM), F32)],
        scratch_shapes=[pltpu.VMEM((nb, HEADS, HEAD_DIM, HEAD_DIM), F32),
                        pltpu.VMEM((nb, T, 2 * W), BF16)],
        compiler_params=_cparams(("parallel", "arbitrary")),
        name="mixer_ab",
    )(x, *consts, c0, n0, m0, s0)


N_CD_CONSTS = 24


def _mixer_cd_kernel(x_ref, *refs, n_seq, **dims):
    consts = refs[:N_CD_CONSTS]
    per_seq = refs[N_CD_CONSTS:]
    h0_ref, conv0_ref, sre0_ref, sim0_ref, _, h_ref, conv_ref, sre_ref, sim_ref = per_seq[:9]

    @pl.when(pl.program_id(1) == 0)
    def _():
        h_ref[...] = h0_ref[...]
        conv_ref[...] = conv0_ref[...]
        sre_ref[...] = sre0_ref[...]
        sim_ref[...] = sim0_ref[...]

    _run_staggered([_mixer_cd_one(x_ref.at[s], *consts, *[r.at[s] for r in per_seq], **dims)
                    for s in range(n_seq)], CD_PHASE_PIECES)


def _mixer_cd_one(x_ref, ng_ref, perm_ref, permt_ref, wcd_ref, bcd_ref, cw_ref, cb_ref, wa_ref, ba_ref,
                  wx_ref, bx_ref, lam_ref, wbu_ref, abr_ref, abi_ref, pr_ref, pi_ref, pwr_ref, pwi_ref,
                  wc_ref, dsk_ref, wglu_ref, bglu_ref, wout_ref, h0_ref, conv0_ref, sre0_ref, sim0_ref,
                  xo_ref, h_ref, conv_ref, sre_ref, sim_ref, ext_ref, lre_ref, lim_ref, sb_ref, hmix_ref,
                  zg_ref, zd_ref, *, rows, c_width, d_width, n_state, gate_tile):
    T = rows
    n = T // 8
    log_n = int(math.log2(n))
    halo = (CONV_W - 1) * 8
    xn = _rms(x_ref[...], ng_ref[...]).astype(BF16)
    xp = _dot(perm_ref[...], xn).astype(BF16)
    yield

    for k in range((2 * c_width + d_width) // gate_tile):
        lo = k * gate_tile
        tile = _dot(xp, wcd_ref[:, lo:lo + gate_tile]) + bcd_ref[:, lo:lo + gate_tile]
        if lo < c_width:
            zg_ref[:, lo:lo + gate_tile] = tile
        elif lo < 2 * c_width:
            ext_ref[halo:halo + T, lo - c_width:lo - c_width + gate_tile] = tile
        else:
            zd_ref[:, lo - 2 * c_width:lo - 2 * c_width + gate_tile] = tile
        yield

    hist = conv_ref[...]
    sub_c = lax.broadcasted_iota(jnp.int32, (8, c_width), 0)
    for k in range(CONV_W - 1):
        g = n - (CONV_W - 1) + k
        grp = ext_ref[halo + g * 8:halo + (g + 1) * 8, :]
        ext_ref[k * 8:(k + 1) * 8, :] = jnp.where(sub_c == 0, hist[k:k + 1, :], pltpu.roll(grp, 1, 0))
        conv_ref[k:k + 1, :] = grp[7:8, :]
    u = cb_ref[...] + cw_ref[CONV_W - 1:CONV_W, :] * ext_ref[halo:halo + T, :]
    for k in range(CONV_W - 1):
        u = u + cw_ref[k:k + 1, :] * ext_ref[k * 8:k * 8 + T, :]
    yield

    ub = u.astype(BF16)
    log_sig_lam = _log_sigmoid(lam_ref[...])
    sub_t = lax.broadcasted_iota(jnp.int32, (8, gate_tile), 0)
    for k in range(c_width // gate_tile):
        cols = slice(k * gate_tile, (k + 1) * gate_tile)
        r = _sigmoid(_dot(ub[:, cols], wa_ref[k]) + ba_ref[:, cols])
        i = _sigmoid(_dot(ub[:, cols], wx_ref[k]) + bx_ref[:, cols])
        log_a = (RG_C * r) * log_sig_lam[:, cols]
        a = jnp.exp(log_a)
        b = jnp.sqrt(-jnp.tanh(log_a) * (a * a + 1.0)) * (i * u[:, cols])
        dec = a[0:8, :]
        loc = b[0:8, :]
        decs, locs = [dec], [loc]
        for j in range(1, n):
            aj = a[j * 8:(j + 1) * 8, :]
            loc = aj * loc + b[j * 8:(j + 1) * 8, :]
            dec = aj * dec
            decs.append(dec)
            locs.append(loc)
        h_prev = h_ref[:, cols]
        ea = dec
        eb = loc + jnp.where(sub_t == 0, dec * h_prev, 0.0)
        for d in (1, 2, 4):
            a_sh = jnp.where(sub_t >= d, pltpu.roll(ea, d, 0), 1.0)
            b_sh = jnp.where(sub_t >= d, pltpu.roll(eb, d, 0), 0.0)
            eb = ea * b_sh + eb
            ea = ea * a_sh
        carry = jnp.where(sub_t == 0, h_prev, pltpu.roll(eb, 1, 0))
        h_ref[:, cols] = eb[7:8, :]
        h = jnp.concatenate([locs[j] + decs[j] * carry for j in range(n)], axis=0)
        hmix_ref[:, cols] = (jax.nn.gelu(zg_ref[:, cols], approximate=True) * h).astype(BF16)
        yield

    bu = _dot(zd_ref[...].astype(BF16), wbu_ref[...])
    p_re = jnp.broadcast_to(abr_ref[...], (8, n_state))
    p_im = jnp.broadcast_to(abi_ref[...], (8, n_state))
    l_re = bu[0:8, :n_state]
    l_im = bu[0:8, n_state:]
    lre_ref[0:8, :] = l_re
    lim_ref[0:8, :] = l_im
    for j in range(1, n):
        rows_j = slice(j * 8, (j + 1) * 8)
        l_re, l_im = (bu[rows_j, :n_state] + (p_re * l_re - p_im * l_im),
                      bu[rows_j, n_state:] + (p_re * l_im + p_im * l_re))
        lre_ref[rows_j, :] = l_re
        lim_ref[rows_j, :] = l_im
        if j == n // 2 or j == n - 1:
            yield
    sub_s = lax.broadcasted_iota(jnp.int32, (8, n_state), 0)
    s_re = sre_ref[...]
    s_im = sim_ref[...]
    q_re = pr_ref[log_n:log_n + 1, :]
    q_im = pi_ref[log_n:log_n + 1, :]
    e_re = l_re + jnp.where(sub_s == 0, q_re * s_re - q_im * s_im, 0.0)
    e_im = l_im + jnp.where(sub_s == 0, q_re * s_im + q_im * s_re, 0.0)
    for k, d in enumerate((1, 2, 4)):
        q_re = pr_ref[log_n + k:log_n + k + 1, :]
        q_im = pi_ref[log_n + k:log_n + k + 1, :]
        r_sh = jnp.where(sub_s >= d, pltpu.roll(e_re, d, 0), 0.0)
        i_sh = jnp.where(sub_s >= d, pltpu.roll(e_im, d, 0), 0.0)
        e_re, e_im = e_re + (q_re * r_sh - q_im * i_sh), e_im + (q_re * i_sh + q_im * r_sh)
    c_re = jnp.where(sub_s == 0, s_re, pltpu.roll(e_re, 1, 0))
    c_im = jnp.where(sub_s == 0, s_im, pltpu.roll(e_im, 1, 0))
    sre_ref[...] = e_re[7:8, :]
    sim_ref[...] = e_im[7:8, :]
    for j in range(0, n, 2):
        f_re, f_im = [], []
        for jj in (j, j + 1):
            rows_j = slice(jj * 8, (jj + 1) * 8)
            w_re = pwr_ref[rows_j, :]
            w_im = pwi_ref[rows_j, :]
            f_re.append(lre_ref[rows_j, :] + (w_re * c_re - w_im * c_im))
            f_im.append(lim_ref[rows_j, :] + (w_re * c_im + w_im * c_re))
        sb_ref[j * 8:(j + 2) * 8, :n_state] = jnp.concatenate(f_re, axis=0).astype(BF16)
        sb_ref[j * 8:(j + 2) * 8, n_state:] = jnp.concatenate(f_im, axis=0).astype(BF16)
    yield
    y = _dot(sb_ref[...], wc_ref[...]) + dsk_ref[...] * zd_ref[...]
    y = jax.nn.gelu(y, approximate=True)
    y = y * _sigmoid(_dot(y.astype(BF16), wglu_ref[...]) + bglu_ref[...])
    hmix_ref[:, c_width:c_width + d_width] = y.astype(BF16)
    yield

    d_model = x_ref.shape[-1]
    mixed = []
    for k in range(d_model // gate_tile):
        cols = slice(k * gate_tile, (k + 1) * gate_tile)
        mixed.append(_dot(permt_ref[...], hmix_ref[:, cols]).astype(BF16))
        yield
    mixed = jnp.concatenate(mixed, axis=1)
    for k in range(d_model // gate_tile):
        cols = slice(k * gate_tile, (k + 1) * gate_tile)
        xo_ref[:, cols] = x_ref[:, cols] + _dot(mixed, wout_ref[:, cols])
        yield


def _mixer_cd(x, ng, w, h0, conv0, sre0, sim0):
    B, L, D = x.shape
    T = min(MIX_ROWS, L)
    c_width = h0.shape[-1]
    n_state = sre0.shape[-1]
    d_width = w["w_glu"].shape[0]
    nb = MIX_SEQS if B % MIX_SEQS == 0 else 1
    kern = functools.partial(_mixer_cd_kernel, n_seq=nb, rows=T, c_width=c_width, d_width=d_width,
                             n_state=n_state, gate_tile=w["w_a"].shape[-1])
    xspec = pl.BlockSpec((nb, T, D), lambda b, l: (b, l, 0))
    st_h = pl.BlockSpec((nb, 1, c_width), lambda b, l: (b, 0, 0))
    st_conv = pl.BlockSpec((nb, CONV_W - 1, c_width), lambda b, l: (b, 0, 0))
    st_s = pl.BlockSpec((nb, 1, n_state), lambda b, l: (b, 0, 0))
    perm, perm_t = _regroup_matrices(T)
    n_pow = (T // 8) * 8
    consts = (ng, perm, perm_t, w["w_cd"], w["b_cd"], w["conv_w"], w["conv_b"], w["w_a"], w["b_a"], w["w_x"],
              w["b_x"], w["lam"], w["w_bu"], w["ab_re"], w["ab_im"], w["pow_re"], w["pow_im"],
              w["run_re"][:n_pow], w["run_im"][:n_pow], w["w_c"], w["d_skip"], w["w_glu"], w["b_glu"], w["w_out"])
    assert len(consts) == N_CD_CONSTS
    return pl.pallas_call(
        kern,
        grid=(B // nb, L // T),
        in_specs=[xspec] + [_const_spec(c.shape) for c in consts] + [st_h, st_conv, st_s, st_s],
        out_specs=[xspec, st_h, st_conv, st_s, st_s],
        out_shape=[jax.ShapeDtypeStruct((B, L, D), F32),
                   jax.ShapeDtypeStruct((B, 1, c_width), F32),
                   jax.ShapeDtypeStruct((B, CONV_W - 1, c_width), F32),
                   jax.ShapeDtypeStruct((B, 1, n_state), F32),
                   jax.ShapeDtypeStruct((B, 1, n_state), F32)],
        scratch_shapes=[pltpu.VMEM((nb, T + (CONV_W - 1) * 8, c_width), F32),
                        pltpu.VMEM((nb, T, n_state), F32),
                        pltpu.VMEM((nb, T, n_state), F32),
                        pltpu.VMEM((nb, T, 2 * n_state), BF16),
                        pltpu.VMEM((nb, T, c_width + d_width), BF16),
                        pltpu.VMEM((nb, T, c_width), F32),
                        pltpu.VMEM((nb, T, d_width), F32)],
        compiler_params=_cparams(("parallel", "arbitrary")),
        name="mixer_cd",
    )(x, *consts, h0, conv0, sre0, sim0)


def _regroup_matrices(rows):
    n = rows // 8
    r = np.arange(rows)
    p = np.zeros((rows, rows), np.float32)
    p[r, (r % 8) * n + r // 8] = 1.0
    return jnp.asarray(p, BF16), jnp.asarray(p.T, BF16)


def _prep_s5_kernel(are_ref, aim_ref, ldt_ref, wbre_ref, wbim_ref, abr_ref, abi_ref, pr_ref, pi_ref,
                    runr_ref, runi_ref, wbu_ref, *, n_state):
    a_re = are_ref[...]
    a_im = aim_ref[...]
    dt = jnp.exp(ldt_ref[...])
    mag = jnp.exp(dt * a_re)
    ab_re = mag * jnp.cos(dt * a_im)
    ab_im = mag * jnp.sin(dt * a_im)
    inv = 1.0 / (a_re * a_re + a_im * a_im)
    z_re = ((ab_re - 1.0) * a_re + ab_im * a_im) * inv
    z_im = (ab_im * a_re - (ab_re - 1.0) * a_im) * inv
    abr_ref[...] = ab_re
    abi_ref[...] = ab_im
    wb_re = wbre_ref[...]
    wb_im = wbim_ref[...]
    wbu_ref[:, :n_state] = (z_re * wb_re - z_im * wb_im).astype(BF16)
    wbu_ref[:, n_state:] = (z_re * wb_im + z_im * wb_re).astype(BF16)
    p_re, p_im = ab_re, ab_im
    for k in range(MAX_SCAN_STEPS):
        pr_ref[k:k + 1, :] = p_re
        pi_ref[k:k + 1, :] = p_im
        p_re, p_im = p_re * p_re - p_im * p_im, 2.0 * (p_re * p_im)
    p_re, p_im = ab_re, ab_im
    for j in range(MIX_ROWS // 8):
        runr_ref[j * 8:(j + 1) * 8, :] = jnp.broadcast_to(p_re, (8, n_state))
        runi_ref[j * 8:(j + 1) * 8, :] = jnp.broadcast_to(p_im, (8, n_state))
        p_re, p_im = p_re * ab_re - p_im * ab_im, p_re * ab_im + p_im * ab_re


def _prep_s5(a_re, a_im, log_dt_rep, wb_re, wb_im):
    n_state = a_re.shape[-1]
    d_width = wb_re.shape[0]
    kern = functools.partial(_prep_s5_kernel, n_state=n_state)
    row = jax.ShapeDtypeStruct((1, n_state), F32)
    tab = jax.ShapeDtypeStruct((MAX_SCAN_STEPS, n_state), F32)
    run = jax.ShapeDtypeStruct((MIX_ROWS, n_state), F32)
    return pl.pallas_call(
        kern,
        out_shape=[row, row, tab, tab, run, run, jax.ShapeDtypeStruct((d_width, 2 * n_state), BF16)],
        name="prep_s5",
    )(a_re, a_im, log_dt_rep, wb_re, wb_im)


def _prep_lb_kernel(logits_ref, lb_ref):
    lg = logits_ref[...]
    n = lg.shape[0]
    mx = lg[0:1, :]
    for i in range(1, n):
        mx = jnp.maximum(mx, lg[i:i + 1, :])
    e = [jnp.exp(lg[i:i + 1, :] - mx) for i in range(n)]
    tot = e[0]
    for i in range(1, n):
        tot = tot + e[i]
    run = None
    for i in range(n):
        p = e[i] / tot
        run = p if run is None else run + p
        lb_ref[i:i + 1, :] = run


def _prep_lb(logits):
    return pl.pallas_call(_prep_lb_kernel, out_shape=jax.ShapeDtypeStruct(logits.shape, F32),
                          name="prep_lb")(logits)


def _block_diag(blocks):
    n, r, c = blocks.shape
    eye = jnp.eye(n, dtype=blocks.dtype)
    return (eye[:, None, :, None] * blocks[:, :, None, :]).reshape(n * r, n * c)


def kernel(x_prompt, x_sample, state_mlstm_C, state_mlstm_n, state_mlstm_m, state_hgrn_S, state_rglru_h,
           state_rglru_conv, state_s5_re, state_s5_im, norm_g, final_norm_g, ffn_w_gate, ffn_w_up, ffn_w_down,
           ab_w_in, ab_b_in, mlstm_norm_g, hgrn_norm_g, hgrn_lb_logits, ab_w_out, cd_w_in, cd_b_in, conv_w,
           conv_b, rg_w_a, rg_b_a, rg_w_x, rg_b_x, rg_lambda, s5_A_re, s5_A_im, s5_log_dt, s5_B_re, s5_B_im,
           s5_C_re, s5_C_im, s5_D, s5_w_glu, s5_b_glu, cd_w_out):
    depth = norm_g.shape[0]
    d_model = x_prompt.shape[-1]
    W = HEADS * HEAD_DIM
    c_width = rg_lambda.shape[-1]
    groups, n_st, s5_ch = s5_B_re.shape[1:]
    n_state = groups * n_st
    d_width = groups * s5_ch
    gate_tile = 256
    blocks_per_tile = gate_tile // rg_w_a.shape[-1]

    lb_all = _prep_lb(hgrn_lb_logits.astype(F32))

    ffn = []
    for l in range(depth):
        ffn.append([(ffn_w_gate[l, i].astype(BF16), ffn_w_up[l, i].astype(BF16), ffn_w_down[l, i].astype(BF16))
                    for i in range(2)])
    mix_w = []
    for l in range(depth):
        j = l // 2
        if l % 2 == 0:
            wi, bi = ab_w_in[j], ab_b_in[j]
            big = jnp.concatenate([wi[:, :4 * W], wi[:, 4 * W + 2 * HEADS:]], axis=1)
            b_big = jnp.concatenate([bi[:4 * W], bi[4 * W + 2 * HEADS:]])[None, :]
            wg = wi[:, 4 * W:4 * W + 2 * HEADS]
            bgate = bi[4 * W:4 * W + 2 * HEADS]
            mix_w.append(dict(
                w_big=big.astype(BF16), b_big=b_big,
                w_gc=jnp.pad(wg, ((0, 0), (0, N_GATE_LANES - 2 * HEADS))).astype(BF16),
                b_gc=jnp.pad(bgate, (0, N_GATE_LANES - 2 * HEADS))[None, :],
                w_gr=jnp.pad(wg.T, ((0, N_GATE_ROWS - 2 * HEADS), (0, 0))).astype(BF16),
                b_gr=jnp.pad(bgate, (0, N_GATE_ROWS - 2 * HEADS))[:, None],
                a_norm_g=mlstm_norm_g[j][None, :], b_norm_g=hgrn_norm_g[j][None, :],
                lb=lb_all[l][None, :], w_out=ab_w_out[j].astype(BF16)))
        else:
            n_tiles = c_width // gate_tile

            def tiles(wb):
                wb = wb.reshape(n_tiles, blocks_per_tile, wb.shape[-2], wb.shape[-1])
                return jnp.stack([_block_diag(wb[t]) for t in range(n_tiles)]).astype(BF16)

            wb_re = _block_diag(jnp.swapaxes(s5_B_re[j], 1, 2))
            wb_im = _block_diag(jnp.swapaxes(s5_B_im[j], 1, 2))
            ab_re, ab_im, pow_re, pow_im, run_re, run_im, w_bu = _prep_s5(
                s5_A_re[j].reshape(1, n_state), s5_A_im[j].reshape(1, n_state),
                jnp.repeat(s5_log_dt[j], n_st)[None, :], wb_re, wb_im)
            w_c = jnp.concatenate([_block_diag(jnp.swapaxes(s5_C_re[j], 1, 2)),
                                   -_block_diag(jnp.swapaxes(s5_C_im[j], 1, 2))], axis=0)
            mix_w.append(dict(
                w_cd=cd_w_in[j].astype(BF16), b_cd=cd_b_in[j][None, :], conv_w=conv_w[j], conv_b=conv_b[j][None, :],
                w_a=tiles(rg_w_a[j]), b_a=rg_b_a[j][None, :], w_x=tiles(rg_w_x[j]), b_x=rg_b_x[j][None, :],
                lam=rg_lambda[j][None, :], w_bu=w_bu, ab_re=ab_re, ab_im=ab_im, pow_re=pow_re, pow_im=pow_im,
                run_re=run_re, run_im=run_im,
                w_c=w_c.astype(BF16), d_skip=s5_D[j][None, :], w_glu=s5_w_glu[j].astype(BF16),
                b_glu=s5_b_glu[j][None, :], w_out=cd_w_out[j].astype(BF16)))
    fg = final_norm_g[None, :]

    def run_trunk(x, states):
        mC, mn, mm, hS, rh, rc, sre, sim = states
        B, L, _ = x.shape
        outs = ([], [], [], [], [], [], [], [])
        for l in range(depth):
            j = l // 2
            wg, wu, wd = ffn[l][0]
            x = _ffn(x.reshape(B * L, d_model), norm_g[l, 0][None, :], wg, wu, wd, fg, False).reshape(B, L, d_model)
            if l % 2 == 0:
                m0 = jnp.broadcast_to(jnp.pad(mm[j], ((0, 0), (0, 8 - HEADS)))[:, :, None], (B, 8, N_GATE_LANES))
                x, c1, n1, m1, s1 = _mixer_ab(x, norm_g[l, 1][None, :], mix_w[l], mC[j], mn[j], m0, hS[j])
                for lst, val in zip(outs[:4], (c1, n1, m1[:, :HEADS, 0], s1)):
                    lst.append(val)
            else:
                x, h1, buf1, re1, im1 = _mixer_cd(x, norm_g[l, 1][None, :], mix_w[l], rh[j][:, None, :], rc[j],
                                                  sre[j].reshape(B, 1, n_state), sim[j].reshape(B, 1, n_state))
                for lst, val in zip(outs[4:], (h1[:, 0], buf1, re1.reshape(B, groups, n_st),
                                               im1.reshape(B, groups, n_st))):
                    lst.append(val)
            wg, wu, wd = ffn[l][1]
            x = _ffn(x.reshape(B * L, d_model), norm_g[l, 2][None, :], wg, wu, wd, fg,
                     l == depth - 1).reshape(B, L, d_model)
        return x, [jnp.stack(lst) for lst in outs]

    n_even = (depth + 1) // 2
    n_odd = depth // 2
    bp = x_prompt.shape[0]
    zero_states = (jnp.zeros((n_even, bp, HEADS, HEAD_DIM, HEAD_DIM), F32),
                   jnp.zeros((n_even, bp, HEADS, HEAD_DIM), F32),
                   jnp.zeros((n_even, bp, HEADS), F32),
                   jnp.zeros((n_even, bp, HEADS, HEAD_DIM, HEAD_DIM), F32),
                   jnp.zeros((n_odd, bp, c_width), F32),
                   jnp.zeros((n_odd, bp, CONV_W - 1, c_width), F32),
                   jnp.zeros((n_odd, bp, groups, n_st), F32),
                   jnp.zeros((n_odd, bp, groups, n_st), F32))
    y_prompt, p_st = run_trunk(x_prompt, zero_states)
    y_sample, s_st = run_trunk(x_sample, (state_mlstm_C, state_mlstm_n, state_mlstm_m, state_hgrn_S,
                                          state_rglru_h, state_rglru_conv, state_s5_re, state_s5_im))
    return (y_prompt, y_sample, *p_st, *s_st)
```

```python
import functools
import math

import jax
import jax.numpy as jnp
import numpy as np
from jax import lax
from jax.experimental import pallas as pl
from jax.experimental.pallas import tpu as pltpu

F32 = jnp.float32
BF16 = jnp.bfloat16

EPS = 1e-6
RG_C = 8.0
CONV_W = 4
HEADS = 4
HEAD_DIM = 128
N_GATE_LANES = 128
V7X_VMEM_LIMIT = 56 * 1024 * 1024
FFN_COLS = 256
FFN_ROWS = 512
FFN_TILES = 2
FFN_STAGGER = 6
MIX_ROWS = 256
HGRN_CHUNK = 64
MAX_SCAN_STEPS = 8
AB_SEQS = 4
AB_STAGGER = 9
OUT_COLS = 256
MIX_SEQS = 4
S5_LANES = 1024
CD_PHASE_PIECES = 8


def _dot(a, b):
    return jnp.dot(a, b, preferred_element_type=F32)


def _dot_nt(a, b):
    return lax.dot_general(a, b, (((1,), (1,)), ((), ())), preferred_element_type=F32)


def _dot_tn(a, b):
    return lax.dot_general(a, b, (((0,), (0,)), ((), ())), preferred_element_type=F32)


def _rms(x, g):
    y = x * lax.rsqrt(jnp.mean(x * x, axis=-1, keepdims=True) + EPS)
    return y * g


def _sigmoid(x):
    return jax.nn.sigmoid(x)


def _log_sigmoid(x):
    return jnp.minimum(x, 0.0) - jnp.log1p(jnp.exp(-jnp.abs(x)))


def _split3(a):
    hi = a.astype(BF16)
    r = a - hi.astype(F32)
    mid = r.astype(BF16)
    lo = (r - mid.astype(F32)).astype(BF16)
    return hi, mid, lo


def _const_spec(shape):
    nd = len(shape)
    return pl.BlockSpec(shape, lambda *_: (0,) * nd, pipeline_mode=pl.Buffered(1))


def _run_staggered(piece_generators, offset):
    pending = list(piece_generators)
    live = []
    tick = 0
    while pending or live:
        if pending and tick % offset == 0:
            live.append(pending.pop(0))
        for g in list(live):
            try:
                next(g)
            except StopIteration:
                live.remove(g)
        tick += 1


def _cparams(sem):
    return pltpu.CompilerParams(dimension_semantics=sem, vmem_limit_bytes=V7X_VMEM_LIMIT)


def _ffn_kernel(x_ref, g_ref, wg_ref, wu_ref, wd_ref, fg_ref, o_ref, acc_ref, *, n_tiles, rows, **kw):
    tiles = [slice(t * rows, (t + 1) * rows) for t in range(n_tiles)]
    _run_staggered([_ffn_tile(x_ref.at[r], g_ref, wg_ref, wu_ref, wd_ref, fg_ref, o_ref.at[r], acc_ref.at[r], **kw)
                    for r in tiles], FFN_STAGGER)


def _ffn_tile(x_ref, g_ref, wg_ref, wu_ref, wd_ref, fg_ref, o_ref, acc_ref, *, n_chunks, final_norm):
    xn = _rms(x_ref[...], g_ref[...]).astype(BF16)
    yield
    for j in range(n_chunks):
        cols = slice(j * FFN_COLS, (j + 1) * FFN_COLS)
        gate = _dot(xn, wg_ref[:, cols])
        up = _dot(xn, wu_ref[:, cols])
        h = ((gate * _sigmoid(gate)) * up).astype(BF16)
        d = _dot(h, wd_ref[cols, :])
        if j == 0:
            acc_ref[...] = d
        else:
            acc_ref[...] += d
        yield
    y = x_ref[...] + 0.5 * acc_ref[...]
    if final_norm:
        y = _rms(y, fg_ref[...])
    o_ref[...] = y


def _ffn(x2, g, wg, wu, wd, fg, final_norm):
    n_tok, d = x2.shape
    rows = min(FFN_ROWS, n_tok)
    n_tiles = FFN_TILES if n_tok % (FFN_TILES * rows) == 0 else 1
    tm = n_tiles * rows
    n_chunks = wg.shape[1] // FFN_COLS
    kern = functools.partial(_ffn_kernel, n_tiles=n_tiles, rows=rows, n_chunks=n_chunks, final_norm=final_norm)
    return pl.pallas_call(
        kern,
        grid=(n_tok // tm,),
        in_specs=[pl.BlockSpec((tm, d), lambda i: (i, 0)),
                  _const_spec(g.shape), _const_spec(wg.shape), _const_spec(wu.shape),
                  _const_spec(wd.shape), _const_spec(fg.shape)],
        out_specs=pl.BlockSpec((tm, d), lambda i: (i, 0)),
        out_shape=jax.ShapeDtypeStruct((n_tok, d), F32),
        scratch_shapes=[pltpu.VMEM((tm, d), F32)],
        compiler_params=_cparams(("parallel",)),
        name="ffn",
    )(x2, g, wg, wu, wd, fg)


N_AB_CONSTS = 9


def _mixer_ab_kernel(x_ref, *refs, n_seq, **dims):
    consts = refs[:N_AB_CONSTS]
    per_seq = refs[N_AB_CONSTS:]
    c0_ref, n0_ref, m0_ref, s0_ref, _, c_ref, n_ref, m_ref, s_ref, st_ref = per_seq[:10]
    li = pl.program_id(1)

    @pl.when(li == 0)
    def _():
        c_ref[...] = c0_ref[...]
        n_ref[...] = n0_ref[...]
        m_ref[...] = m0_ref[...]
        for s in range(n_seq):
            for h in range(HEADS):
                st_ref[s, h] = s0_ref[s, h].T

    _run_staggered([_mixer_ab_one(x_ref.at[s], *consts, *[r.at[s] for r in per_seq], **dims)
                    for s in range(n_seq)], AB_STAGGER)

    @pl.when(li == pl.num_programs(1) - 1)
    def _():
        for s in range(n_seq):
            for h in range(HEADS):
                s_ref[s, h] = st_ref[s, h].T


def _mixer_ab_one(x_ref, ng_ref, wbig_ref, bbig_ref, wgc_ref, bgc_ref,
                  ang_ref, bng_ref, lb_ref, wout_ref, c0_ref, n0_ref, m0_ref, s0_ref,
                  xo_ref, c_ref, n_ref, m_ref, s_ref, st_ref, hmix_ref, *, rows, hchunk):
    T = rows
    W = HEADS * HEAD_DIM
    x = x_ref[...]
    xn = _rms(x, ng_ref[...]).astype(BF16)
    zc = _dot(xn, wgc_ref[...]) + bgc_ref[...]
    yield

    z = []
    for k in range(8):
        halves = []
        for c0 in (k * W, k * W + W // 2):
            halves.append(_dot(xn, wbig_ref[:, c0:c0 + W // 2]) + bbig_ref[:, c0:c0 + W // 2])
            yield
        z.append(jnp.concatenate(halves, axis=1))

    aq = z[0] * (HEAD_DIM ** -0.5)
    ak, av, ao = z[1], z[2], z[3]
    ri = lax.broadcasted_iota(jnp.int32, (T, T), 0)
    ci = lax.broadcasted_iota(jnp.int32, (T, T), 1)
    causal = ri >= ci
    tril = jnp.where(causal, 1.0, 0.0).astype(BF16)
    fc = sum(_dot(tril, p) for p in _split3(_log_sigmoid(zc)))
    zr = zc.T
    fr = fc.T
    yield
    for h in range(HEADS):
        cols = slice(h * HEAD_DIM, (h + 1) * HEAD_DIM)
        ig_c = zc[:, h:h + 1]
        ig_r = zr[h:h + 1, :]
        f_c = fc[:, HEADS + h:HEADS + h + 1]
        f_r = fr[HEADS + h:HEADS + h + 1, :]
        m_prev = m_ref[h:h + 1, 0:1]
        log_src = jnp.where(causal, f_c + (ig_r - f_r), -jnp.inf)
        log_prev = f_c + m_prev
        m_t = jnp.maximum(log_prev, jnp.max(log_src, axis=-1, keepdims=True))
        w_src = jnp.exp(log_src - m_t)
        w_prev = jnp.exp(log_prev - m_t)
        qf = aq[:, cols]
        kf = ak[:, cols]
        q = qf.astype(BF16)
        v = av[:, cols].astype(BF16)
        s = _dot_nt(q, kf.astype(BF16)) * w_src
        yield
        c_old = c_ref[h]
        n_old = n_ref[h:h + 1, :]
        num = _dot(s.astype(BF16), v) + w_prev * _dot(q, c_old.astype(BF16))
        den = jnp.sum(s, axis=-1, keepdims=True) + w_prev * jnp.sum(qf * n_old, axis=-1, keepdims=True)
        den = jnp.maximum(jnp.abs(den), jnp.exp(-m_t))
        hh = num / den
        f_last = f_c[T - 1:T, :]
        m_new = m_t[T - 1:T, :]
        decay = jnp.exp(f_last + m_prev - m_new)
        w_end = jnp.exp(f_last - f_c + ig_c - m_new)
        wk = w_end * kf
        c_ref[h] = decay * c_old + _dot_tn(wk.astype(BF16), v)
        n_ref[h:h + 1, :] = decay * n_old + jnp.sum(wk, axis=0, keepdims=True)
        m_ref[h:h + 1, :] = jnp.broadcast_to(m_new, (1, N_GATE_LANES))
        y = hh * lax.rsqrt(jnp.mean(hh * hh, axis=-1, keepdims=True) + EPS)
        out = (y * ang_ref[:, cols]) * _sigmoid(ao[:, cols])
        hmix_ref[:, cols] = out.astype(BF16)
        yield

    qh = z[4] * _sigmoid(z[4])
    zf, iv, bg = z[5], z[6], z[7]
    lb = lb_ref[...]
    lf = jnp.log(lb + (1.0 - lb) * _sigmoid(zf))
    kk = (1.0 - lb) * _sigmoid(-zf)
    sh = int(math.log2(hchunk))
    same_chunk = lax.shift_right_logical(ri, sh) == lax.shift_right_logical(ci, sh)
    blocktril = jnp.where(causal & same_chunk, 1.0, 0.0).astype(BF16)
    G = sum(_dot(blocktril, p) for p in _split3(lf))
    q_dec = (qh * jnp.exp(G)).astype(BF16)
    k_inv = (kk * jnp.exp(-G)).astype(BF16)
    ivb = iv.astype(BF16)
    yield
    rs = lax.broadcasted_iota(jnp.int32, (hchunk, hchunk), 0)
    cs = lax.broadcasted_iota(jnp.int32, (hchunk, hchunk), 1)
    sub_causal = rs >= cs
    for j in range(T // hchunk):
        r = slice(j * hchunk, (j + 1) * hchunk)
        g_end = G[(j + 1) * hchunk - 1:(j + 1) * hchunk, :]
        k_end = (kk[r, :] * jnp.exp(g_end - G[r, :])).astype(BF16)
        e_end = jnp.exp(g_end)
        for h in range(HEADS):
            cols = slice(h * HEAD_DIM, (h + 1) * HEAD_DIM)
            qd = q_dec[r, cols]
            att = jnp.where(sub_causal, _dot_nt(qd, k_inv[r, cols]), 0.0)
            st_old = st_ref[h]
            o = _dot(att.astype(BF16), ivb[r, cols]) + _dot_nt(qd, st_old.astype(BF16))
            st_ref[h] = st_old * e_end[:, cols] + _dot_tn(ivb[r, cols], k_end[:, cols])
            y = o * lax.rsqrt(jnp.mean(o * o, axis=-1, keepdims=True) + EPS)
            gate = bg[r, cols]
            out = (y * bng_ref[:, cols]) * (gate * _sigmoid(gate))
            hmix_ref[r, W + h * HEAD_DIM:W + (h + 1) * HEAD_DIM] = out.astype(BF16)
            if h % 2 == 1:
                yield

    d_model = x.shape[-1]
    for k in range(d_model // OUT_COLS):
        cols = slice(k * OUT_COLS, (k + 1) * OUT_COLS)
        xo_ref[:, cols] = x[:, cols] + _dot(hmix_ref[...], wout_ref[:, cols])
        yield


def _mixer_ab(x, ng, w, c0, n0, m0, s0):
    B, L, D = x.shape
    T = min(MIX_ROWS, L)
    hchunk = HGRN_CHUNK if T % HGRN_CHUNK == 0 else T
    W = HEADS * HEAD_DIM
    nb = AB_SEQS if B % AB_SEQS == 0 else 1
    kern = functools.partial(_mixer_ab_kernel, n_seq=nb, rows=T, hchunk=hchunk)
    state4 = pl.BlockSpec((nb, HEADS, HEAD_DIM, HEAD_DIM), lambda b, l: (b, 0, 0, 0))
    state_n = pl.BlockSpec((nb, HEADS, HEAD_DIM), lambda b, l: (b, 0, 0))
    state_m = pl.BlockSpec((nb, 8, N_GATE_LANES), lambda b, l: (b, 0, 0))
    xspec = pl.BlockSpec((nb, T, D), lambda b, l: (b, l, 0))
    consts = (ng, w["w_big"], w["b_big"], w["w_gc"], w["b_gc"],
              w["a_norm_g"], w["b_norm_g"], w["lb"], w["w_out"])
    assert len(consts) == N_AB_CONSTS
    return pl.pallas_call(
        kern,
        grid=(B // nb, L // T),
        in_specs=[xspec] + [_const_spec(c.shape) for c in consts] + [state4, state_n, state_m, state4],
        out_specs=[xspec, state4, state_n, state_m, state4],
        out_shape=[jax.ShapeDtypeStruct((B, L, D), F32),
                   jax.ShapeDtypeStruct((B, HEADS, HEAD_DIM, HEAD_DIM), F32),
                   jax.ShapeDtypeStruct((B, HEADS, HEAD_DIM), F32),
                   jax.ShapeDtypeStruct((B, 8, N_GATE_LANES), F32),
                   jax.ShapeDtypeStruct((B, HEADS, HEAD_DIM, HEAD_DIM), F32)],
        scratch_shapes=[pltpu.VMEM((nb, HEADS, HEAD_DIM, HEAD_DIM), F32),
                        pltpu.VMEM((nb, T, 2 * W), BF16)],
        compiler_params=_cparams(("parallel", "arbitrary")),
        name="mixer_ab",
    )(x, *consts, c0, n0, m0, s0)


N_CD_CONSTS = 24


def _mixer_cd_kernel(x_ref, *refs, n_seq, **dims):
    consts = refs[:N_CD_CONSTS]
    per_seq = refs[N_CD_CONSTS:]
    h0_ref, conv0_ref, sre0_ref, sim0_ref, _, h_ref, conv_ref, sre_ref, sim_ref = per_seq[:9]

    @pl.when(pl.program_id(1) == 0)
    def _():
        h_ref[...] = h0_ref[...]
        conv_ref[...] = conv0_ref[...]
        sre_ref[...] = sre0_ref[...]
        sim_ref[...] = sim0_ref[...]

    _run_staggered([_mixer_cd_one(x_ref.at[s], *consts, *[r.at[s] for r in per_seq], **dims)
                    for s in range(n_seq)], CD_PHASE_PIECES)


def _mixer_cd_one(x_ref, ng_ref, perm_ref, permt_ref, wcd_ref, bcd_ref, cw_ref, cb_ref, wa_ref, ba_ref,
                  wx_ref, bx_ref, lam_ref, wbu_ref, abr_ref, abi_ref, pr_ref, pi_ref, pwr_ref, pwi_ref,
                  wc_ref, dsk_ref, wglu_ref, bglu_ref, wout_ref, h0_ref, conv0_ref, sre0_ref, sim0_ref,
                  xo_ref, h_ref, conv_ref, sre_ref, sim_ref, ext_ref, lre_ref, lim_ref, sb_ref, hmix_ref,
                  zg_ref, zd_ref, *, rows, c_width, d_width, n_state, gate_tile):
    T = rows
    n = T // 8
    log_n = int(math.log2(n))
    halo = (CONV_W - 1) * 8
    xn = _rms(x_ref[...], ng_ref[...]).astype(BF16)
    xp = _dot(perm_ref[...], xn).astype(BF16)
    yield

    for k in range((2 * c_width + d_width) // gate_tile):
        lo = k * gate_tile
        tile = _dot(xp, wcd_ref[:, lo:lo + gate_tile]) + bcd_ref[:, lo:lo + gate_tile]
        if lo < c_width:
            zg_ref[:, lo:lo + gate_tile] = tile
        elif lo < 2 * c_width:
            ext_ref[halo:halo + T, lo - c_width:lo - c_width + gate_tile] = tile
        else:
            zd_ref[:, lo - 2 * c_width:lo - 2 * c_width + gate_tile] = tile
        yield

    hist = conv_ref[...]
    sub_c = lax.broadcasted_iota(jnp.int32, (8, c_width), 0)
    for k in range(CONV_W - 1):
        g = n - (CONV_W - 1) + k
        grp = ext_ref[halo + g * 8:halo + (g + 1) * 8, :]
        ext_ref[k * 8:(k + 1) * 8, :] = jnp.where(sub_c == 0, hist[k:k + 1, :], pltpu.roll(grp, 1, 0))
        conv_ref[k:k + 1, :] = grp[7:8, :]
    u = cb_ref[...] + cw_ref[CONV_W - 1:CONV_W, :] * ext_ref[halo:halo + T, :]
    for k in range(CONV_W - 1):
        u = u + cw_ref[k:k + 1, :] * ext_ref[k * 8:k * 8 + T, :]
    yield

    ub = u.astype(BF16)
    log_sig_lam = _log_sigmoid(lam_ref[...])
    sub_t = lax.broadcasted_iota(jnp.int32, (8, gate_tile), 0)
    for k in range(c_width // gate_tile):
        cols = slice(k * gate_tile, (k + 1) * gate_tile)
        r = _sigmoid(_dot(ub[:, cols], wa_ref[k]) + ba_ref[:, cols])
        i = _sigmoid(_dot(ub[:, cols], wx_ref[k]) + bx_ref[:, cols])
        log_a = (RG_C * r) * log_sig_lam[:, cols]
        a = jnp.exp(log_a)
        b = jnp.sqrt(-jnp.tanh(log_a) * (a * a + 1.0)) * (i * u[:, cols])
        dec = a[0:8, :]
        loc = b[0:8, :]
        decs, locs = [dec], [loc]
        for j in range(1, n):
            aj = a[j * 8:(j + 1) * 8, :]
            loc = aj * loc + b[j * 8:(j + 1) * 8, :]
            dec = aj * dec
            decs.append(dec)
            locs.append(loc)
        h_prev = h_ref[:, cols]
        ea = dec
        eb = loc + jnp.where(sub_t == 0, dec * h_prev, 0.0)
        for d in (1, 2, 4):
            a_sh = jnp.where(sub_t >= d, pltpu.roll(ea, d, 0), 1.0)
            b_sh = jnp.where(sub_t >= d, pltpu.roll(eb, d, 0), 0.0)
            eb = ea * b_sh + eb
            ea = ea * a_sh
        carry = jnp.where(sub_t == 0, h_prev, pltpu.roll(eb, 1, 0))
        h_ref[:, cols] = eb[7:8, :]
        h = jnp.concatenate([locs[j] + decs[j] * carry for j in range(n)], axis=0)
        hmix_ref[:, cols] = (jax.nn.gelu(zg_ref[:, cols], approximate=True) * h).astype(BF16)
        yield

    bu = _dot(zd_ref[...].astype(BF16), wbu_ref[...])
    for lo in range(0, n_state, S5_LANES):
        re_cols = slice(lo, lo + S5_LANES)
        im_cols = slice(n_state + lo, n_state + lo + S5_LANES)
        p_re = jnp.broadcast_to(abr_ref[:, re_cols], (8, S5_LANES))
        p_im = jnp.broadcast_to(abi_ref[:, re_cols], (8, S5_LANES))
        l_re = bu[0:8, re_cols]
        l_im = bu[0:8, im_cols]
        lre_ref[0:8, re_cols] = l_re
        lim_ref[0:8, re_cols] = l_im
        for j in range(1, n):
            rows_j = slice(j * 8, (j + 1) * 8)
            l_re, l_im = (bu[rows_j, re_cols] + (p_re * l_re - p_im * l_im),
                          bu[rows_j, im_cols] + (p_re * l_im + p_im * l_re))
            lre_ref[rows_j, re_cols] = l_re
            lim_ref[rows_j, re_cols] = l_im
        yield
        sub_s = lax.broadcasted_iota(jnp.int32, (8, S5_LANES), 0)
        s_re = sre_ref[:, re_cols]
        s_im = sim_ref[:, re_cols]
        q_re = pr_ref[log_n:log_n + 1, re_cols]
        q_im = pi_ref[log_n:log_n + 1, re_cols]
        e_re = l_re + jnp.where(sub_s == 0, q_re * s_re - q_im * s_im, 0.0)
        e_im = l_im + jnp.where(sub_s == 0, q_re * s_im + q_im * s_re, 0.0)
        for k, d in enumerate((1, 2, 4)):
            q_re = pr_ref[log_n + k:log_n + k + 1, re_cols]
            q_im = pi_ref[log_n + k:log_n + k + 1, re_cols]
            r_sh = jnp.where(sub_s >= d, pltpu.roll(e_re, d, 0), 0.0)
            i_sh = jnp.where(sub_s >= d, pltpu.roll(e_im, d, 0), 0.0)
            e_re, e_im = e_re + (q_re * r_sh - q_im * i_sh), e_im + (q_re * i_sh + q_im * r_sh)
        c_re = jnp.where(sub_s == 0, s_re, pltpu.roll(e_re, 1, 0))
        c_im = jnp.where(sub_s == 0, s_im, pltpu.roll(e_im, 1, 0))
        sre_ref[:, re_cols] = e_re[7:8, :]
        sim_ref[:, re_cols] = e_im[7:8, :]
        for j in range(0, n, 2):
            f_re, f_im = [], []
            for jj in (j, j + 1):
                rows_j = slice(jj * 8, (jj + 1) * 8)
                w_re = pwr_ref[rows_j, re_cols]
                w_im = pwi_ref[rows_j, re_cols]
                f_re.append(lre_ref[rows_j, re_cols] + (w_re * c_re - w_im * c_im))
                f_im.append(lim_ref[rows_j, re_cols] + (w_re * c_im + w_im * c_re))
            sb_ref[j * 8:(j + 2) * 8, re_cols] = jnp.concatenate(f_re, axis=0).astype(BF16)
            sb_ref[j * 8:(j + 2) * 8, im_cols] = jnp.concatenate(f_im, axis=0).astype(BF16)
        yield
    y = _dot(sb_ref[...], wc_ref[...]) + dsk_ref[...] * zd_ref[...]
    y = jax.nn.gelu(y, approximate=True)
    y = y * _sigmoid(_dot(y.astype(BF16), wglu_ref[...]) + bglu_ref[...])
    hmix_ref[:, c_width:c_width + d_width] = y.astype(BF16)
    yield

    d_model = x_ref.shape[-1]
    mixed = []
    for k in range(d_model // gate_tile):
        cols = slice(k * gate_tile, (k + 1) * gate_tile)
        mixed.append(_dot(permt_ref[...], hmix_ref[:, cols]).astype(BF16))
        yield
    mixed = jnp.concatenate(mixed, axis=1)
    for k in range(d_model // gate_tile):
        cols = slice(k * gate_tile, (k + 1) * gate_tile)
        xo_ref[:, cols] = x_ref[:, cols] + _dot(mixed, wout_ref[:, cols])
        yield


def _mixer_cd(x, ng, w, h0, conv0, sre0, sim0):
    B, L, D = x.shape
    T = min(MIX_ROWS, L)
    c_width = h0.shape[-1]
    n_state = sre0.shape[-1]
    d_width = w["w_glu"].shape[0]
    nb = MIX_SEQS if B % MIX_SEQS == 0 else 1
    kern = functools.partial(_mixer_cd_kernel, n_seq=nb, rows=T, c_width=c_width, d_width=d_width,
                             n_state=n_state, gate_tile=w["w_a"].shape[-1])
    xspec = pl.BlockSpec((nb, T, D), lambda b, l: (b, l, 0))
    st_h = pl.BlockSpec((nb, 1, c_width), lambda b, l: (b, 0, 0))
    st_conv = pl.BlockSpec((nb, CONV_W - 1, c_width), lambda b, l: (b, 0, 0))
    st_s = pl.BlockSpec((nb, 1, n_state), lambda b, l: (b, 0, 0))
    perm, perm_t = _regroup_matrices(T)
    n_pow = (T // 8) * 8
    consts = (ng, perm, perm_t, w["w_cd"], w["b_cd"], w["conv_w"], w["conv_b"], w["w_a"], w["b_a"], w["w_x"],
              w["b_x"], w["lam"], w["w_bu"], w["ab_re"], w["ab_im"], w["pow_re"], w["pow_im"],
              w["run_re"][:n_pow], w["run_im"][:n_pow], w["w_c"], w["d_skip"], w["w_glu"], w["b_glu"], w["w_out"])
    assert len(consts) == N_CD_CONSTS
    return pl.pallas_call(
        kern,
        grid=(B // nb, L // T),
        in_specs=[xspec] + [_const_spec(c.shape) for c in consts] + [st_h, st_conv, st_s, st_s],
        out_specs=[xspec, st_h, st_conv, st_s, st_s],
        out_shape=[jax.ShapeDtypeStruct((B, L, D), F32),
                   jax.ShapeDtypeStruct((B, 1, c_width), F32),
                   jax.ShapeDtypeStruct((B, CONV_W - 1, c_width), F32),
                   jax.ShapeDtypeStruct((B, 1, n_state), F32),
                   jax.ShapeDtypeStruct((B, 1, n_state), F32)],
        scratch_shapes=[pltpu.VMEM((nb, T + (CONV_W - 1) * 8, c_width), F32),
                        pltpu.VMEM((nb, T, n_state), F32),
                        pltpu.VMEM((nb, T, n_state), F32),
                        pltpu.VMEM((nb, T, 2 * n_state), BF16),
                        pltpu.VMEM((nb, T, c_width + d_width), BF16),
                        pltpu.VMEM((nb, T, c_width), F32),
                        pltpu.VMEM((nb, T, d_width), F32)],
        compiler_params=_cparams(("parallel", "arbitrary")),
        name="mixer_cd",
    )(x, *consts, h0, conv0, sre0, sim0)


def _regroup_matrices(rows):
    n = rows // 8
    r = np.arange(rows)
    p = np.zeros((rows, rows), np.float32)
    p[r, (r % 8) * n + r // 8] = 1.0
    return jnp.asarray(p, BF16), jnp.asarray(p.T, BF16)


def _prep_s5_kernel(are_ref, aim_ref, ldt_ref, wbre_ref, wbim_ref, abr_ref, abi_ref, pr_ref, pi_ref,
                    runr_ref, runi_ref, wbu_ref, *, n_state):
    a_re = are_ref[...]
    a_im = aim_ref[...]
    dt = jnp.exp(ldt_ref[...])
    mag = jnp.exp(dt * a_re)
    ab_re = mag * jnp.cos(dt * a_im)
    ab_im = mag * jnp.sin(dt * a_im)
    inv = 1.0 / (a_re * a_re + a_im * a_im)
    z_re = ((ab_re - 1.0) * a_re + ab_im * a_im) * inv
    z_im = (ab_im * a_re - (ab_re - 1.0) * a_im) * inv
    abr_ref[...] = ab_re
    abi_ref[...] = ab_im
    wb_re = wbre_ref[...]
    wb_im = wbim_ref[...]
    wbu_ref[:, :n_state] = (z_re * wb_re - z_im * wb_im).astype(BF16)
    wbu_ref[:, n_state:] = (z_re * wb_im + z_im * wb_re).astype(BF16)
    p_re, p_im = ab_re, ab_im
    for k in range(MAX_SCAN_STEPS):
        pr_ref[k:k + 1, :] = p_re
        pi_ref[k:k + 1, :] = p_im
        p_re, p_im = p_re * p_re - p_im * p_im, 2.0 * (p_re * p_im)
    p_re, p_im = ab_re, ab_im
    for j in range(MIX_ROWS // 8):
        runr_ref[j * 8:(j + 1) * 8, :] = jnp.broadcast_to(p_re, (8, n_state))
        runi_ref[j * 8:(j + 1) * 8, :] = jnp.broadcast_to(p_im, (8, n_state))
        p_re, p_im = p_re * ab_re - p_im * ab_im, p_re * ab_im + p_im * ab_re


def _prep_s5(a_re, a_im, log_dt_rep, wb_re, wb_im):
    n_state = a_re.shape[-1]
    d_width = wb_re.shape[0]
    kern = functools.partial(_prep_s5_kernel, n_state=n_state)
    row = jax.ShapeDtypeStruct((1, n_state), F32)
    tab = jax.ShapeDtypeStruct((MAX_SCAN_STEPS, n_state), F32)
    run = jax.ShapeDtypeStruct((MIX_ROWS, n_state), F32)
    return pl.pallas_call(
        kern,
        out_shape=[row, row, tab, tab, run, run, jax.ShapeDtypeStruct((d_width, 2 * n_state), BF16)],
        name="prep_s5",
    )(a_re, a_im, log_dt_rep, wb_re, wb_im)


def _prep_lb_kernel(logits_ref, lb_ref):
    lg = logits_ref[...]
    n = lg.shape[0]
    mx = lg[0:1, :]
    for i in range(1, n):
        mx = jnp.maximum(mx, lg[i:i + 1, :])
    e = [jnp.exp(lg[i:i + 1, :] - mx) for i in range(n)]
    tot = e[0]
    for i in range(1, n):
        tot = tot + e[i]
    run = None
    for i in range(n):
        p = e[i] / tot
        run = p if run is None else run + p
        lb_ref[i:i + 1, :] = run


def _prep_lb(logits):
    return pl.pallas_call(_prep_lb_kernel, out_shape=jax.ShapeDtypeStruct(logits.shape, F32),
                          name="prep_lb")(logits)


def _block_diag(blocks):
    n, r, c = blocks.shape
    eye = jnp.eye(n, dtype=blocks.dtype)
    return (eye[:, None, :, None] * blocks[:, :, None, :]).reshape(n * r, n * c)


def kernel(x_prompt, x_sample, state_mlstm_C, state_mlstm_n, state_mlstm_m, state_hgrn_S, state_rglru_h,
           state_rglru_conv, state_s5_re, state_s5_im, norm_g, final_norm_g, ffn_w_gate, ffn_w_up, ffn_w_down,
           ab_w_in, ab_b_in, mlstm_norm_g, hgrn_norm_g, hgrn_lb_logits, ab_w_out, cd_w_in, cd_b_in, conv_w,
           conv_b, rg_w_a, rg_b_a, rg_w_x, rg_b_x, rg_lambda, s5_A_re, s5_A_im, s5_log_dt, s5_B_re, s5_B_im,
           s5_C_re, s5_C_im, s5_D, s5_w_glu, s5_b_glu, cd_w_out):
    depth = norm_g.shape[0]
    d_model = x_prompt.shape[-1]
    W = HEADS * HEAD_DIM
    c_width = rg_lambda.shape[-1]
    groups, n_st, s5_ch = s5_B_re.shape[1:]
    n_state = groups * n_st
    d_width = groups * s5_ch
    gate_tile = 256
    blocks_per_tile = gate_tile // rg_w_a.shape[-1]

    lb_all = _prep_lb(hgrn_lb_logits.astype(F32))

    ffn = []
    for l in range(depth):
        ffn.append([(ffn_w_gate[l, i].astype(BF16), ffn_w_up[l, i].astype(BF16), ffn_w_down[l, i].astype(BF16))
                    for i in range(2)])
    mix_w = []
    for l in range(depth):
        j = l // 2
        if l % 2 == 0:
            wi, bi = ab_w_in[j], ab_b_in[j]
            big = jnp.concatenate([wi[:, :4 * W], wi[:, 4 * W + 2 * HEADS:]], axis=1)
            b_big = jnp.concatenate([bi[:4 * W], bi[4 * W + 2 * HEADS:]])[None, :]
            wg = wi[:, 4 * W:4 * W + 2 * HEADS]
            bgate = bi[4 * W:4 * W + 2 * HEADS]
            mix_w.append(dict(
                w_big=big.astype(BF16), b_big=b_big,
                w_gc=jnp.pad(wg, ((0, 0), (0, N_GATE_LANES - 2 * HEADS))).astype(BF16),
                b_gc=jnp.pad(bgate, (0, N_GATE_LANES - 2 * HEADS))[None, :],
                a_norm_g=mlstm_norm_g[j][None, :], b_norm_g=hgrn_norm_g[j][None, :],
                lb=lb_all[l][None, :], w_out=ab_w_out[j].astype(BF16)))
        else:
            n_tiles = c_width // gate_tile

            def tiles(wb):
                wb = wb.reshape(n_tiles, blocks_per_tile, wb.shape[-2], wb.shape[-1])
                return jnp.stack([_block_diag(wb[t]) for t in range(n_tiles)]).astype(BF16)

            wb_re = _block_diag(jnp.swapaxes(s5_B_re[j], 1, 2))
            wb_im = _block_diag(jnp.swapaxes(s5_B_im[j], 1, 2))
            ab_re, ab_im, pow_re, pow_im, run_re, run_im, w_bu = _prep_s5(
                s5_A_re[j].reshape(1, n_state), s5_A_im[j].reshape(1, n_state),
                jnp.repeat(s5_log_dt[j], n_st)[None, :], wb_re, wb_im)
            w_c = jnp.concatenate([_block_diag(jnp.swapaxes(s5_C_re[j], 1, 2)),
                                   -_block_diag(jnp.swapaxes(s5_C_im[j], 1, 2))], axis=0)
            mix_w.append(dict(
                w_cd=cd_w_in[j].astype(BF16), b_cd=cd_b_in[j][None, :], conv_w=conv_w[j], conv_b=conv_b[j][None, :],
                w_a=tiles(rg_w_a[j]), b_a=rg_b_a[j][None, :], w_x=tiles(rg_w_x[j]), b_x=rg_b_x[j][None, :],
                lam=rg_lambda[j][None, :], w_bu=w_bu, ab_re=ab_re, ab_im=ab_im, pow_re=pow_re, pow_im=pow_im,
                run_re=run_re, run_im=run_im,
                w_c=w_c.astype(BF16), d_skip=s5_D[j][None, :], w_glu=s5_w_glu[j].astype(BF16),
                b_glu=s5_b_glu[j][None, :], w_out=cd_w_out[j].astype(BF16)))
    fg = final_norm_g[None, :]

    def run_trunk(x, states):
        mC, mn, mm, hS, rh, rc, sre, sim = states
        B, L, _ = x.shape
        outs = ([], [], [], [], [], [], [], [])
        for l in range(depth):
            j = l // 2
            wg, wu, wd = ffn[l][0]
            x = _ffn(x.reshape(B * L, d_model), norm_g[l, 0][None, :], wg, wu, wd, fg, False).reshape(B, L, d_model)
            if l % 2 == 0:
                m0 = jnp.broadcast_to(jnp.pad(mm[j], ((0, 0), (0, 8 - HEADS)))[:, :, None], (B, 8, N_GATE_LANES))
                x, c1, n1, m1, s1 = _mixer_ab(x, norm_g[l, 1][None, :], mix_w[l], mC[j], mn[j], m0, hS[j])
                for lst, val in zip(outs[:4], (c1, n1, m1[:, :HEADS, 0], s1)):
                    lst.append(val)
            else:
                x, h1, buf1, re1, im1 = _mixer_cd(x, norm_g[l, 1][None, :], mix_w[l], rh[j][:, None, :], rc[j],
                                                  sre[j].reshape(B, 1, n_state), sim[j].reshape(B, 1, n_state))
                for lst, val in zip(outs[4:], (h1[:, 0], buf1, re1.reshape(B, groups, n_st),
                                               im1.reshape(B, groups, n_st))):
                    lst.append(val)
            wg, wu, wd = ffn[l][1]
            x = _ffn(x.reshape(B * L, d_model), norm_g[l, 2][None, :], wg, wu, wd, fg,
                     l == depth - 1).reshape(B, L, d_model)
        return x, [jnp.stack(lst) for lst in outs]

    n_even = (depth + 1) // 2
    n_odd = depth // 2
    bp = x_prompt.shape[0]
    zero_states = (jnp.zeros((n_even, bp, HEADS, HEAD_DIM, HEAD_DIM), F32),
                   jnp.zeros((n_even, bp, HEADS, HEAD_DIM), F32),
                   jnp.zeros((n_even, bp, HEADS), F32),
                   jnp.zeros((n_even, bp, HEADS, HEAD_DIM, HEAD_DIM), F32),
                   jnp.zeros((n_odd, bp, c_width), F32),
                   jnp.zeros((n_odd, bp, CONV_W - 1, c_width), F32),
                   jnp.zeros((n_odd, bp, groups, n_st), F32),
                   jnp.zeros((n_odd, bp, groups, n_st), F32))
    y_prompt, p_st = run_trunk(x_prompt, zero_states)
    y_sample, s_st = run_trunk(x_sample, (state_mlstm_C, state_mlstm_n, state_mlstm_m, state_hgrn_S,
                                          state_rglru_h, state_rglru_conv, state_s5_re, state_s5_im))
    return (y_prompt, y_sample, *p_st, *s_st)
```

```python
import functools
import math

import jax
import jax.numpy as jnp
import numpy as np
from jax import lax
from jax.experimental import pallas as pl
from jax.experimental.pallas import tpu as pltpu

F32 = jnp.float32
BF16 = jnp.bfloat16

EPS = 1e-6
RG_C = 8.0
CONV_W = 4
HEADS = 4
HEAD_DIM = 128
N_GATE_LANES = 128
V7X_VMEM_LIMIT = 56 * 1024 * 1024
FFN_COLS = 256
FFN_ROWS = 512
FFN_TILES = 2
FFN_STAGGER = 6
MIX_ROWS = 256
HGRN_CHUNK = 64
MAX_SCAN_STEPS = 8
MIX_STEP_ROWS = 1024
MAX_STEP_SEQS = 4
AB_STAGGER = 9
OUT_COLS = 256
S5_LANES = 1024
CD_PHASE_PIECES = 8


def _dot(a, b):
    return jnp.dot(a, b, preferred_element_type=F32)


def _dot_nt(a, b):
    return lax.dot_general(a, b, (((1,), (1,)), ((), ())), preferred_element_type=F32)


def _dot_tn(a, b):
    return lax.dot_general(a, b, (((0,), (0,)), ((), ())), preferred_element_type=F32)


def _rms(x, g):
    y = x * lax.rsqrt(jnp.mean(x * x, axis=-1, keepdims=True) + EPS)
    return y * g


def _sigmoid(x):
    return jax.nn.sigmoid(x)


def _log_sigmoid(x):
    return jnp.minimum(x, 0.0) - jnp.log1p(jnp.exp(-jnp.abs(x)))


def _split3(a):
    hi = a.astype(BF16)
    r = a - hi.astype(F32)
    mid = r.astype(BF16)
    lo = (r - mid.astype(F32)).astype(BF16)
    return hi, mid, lo


def _const_spec(shape):
    nd = len(shape)
    return pl.BlockSpec(shape, lambda *_: (0,) * nd, pipeline_mode=pl.Buffered(1))


def _run_staggered(piece_generators, offset):
    pending = list(piece_generators)
    live = []
    tick = 0
    while pending or live:
        if pending and tick % offset == 0:
            live.append(pending.pop(0))
        for g in list(live):
            try:
                next(g)
            except StopIteration:
                live.remove(g)
        tick += 1


def _seqs_per_step(n_seqs, rows):
    want = max(1, min(MAX_STEP_SEQS, MIX_STEP_ROWS // rows))
    return max(d for d in range(1, n_seqs + 1) if n_seqs % d == 0 and d <= want)


def _cparams(sem):
    return pltpu.CompilerParams(dimension_semantics=sem, vmem_limit_bytes=V7X_VMEM_LIMIT)


def _ffn_kernel(x_ref, g_ref, wg_ref, wu_ref, wd_ref, fg_ref, o_ref, acc_ref, *, n_tiles, rows, **kw):
    tiles = [slice(t * rows, (t + 1) * rows) for t in range(n_tiles)]
    _run_staggered([_ffn_tile(x_ref.at[r], g_ref, wg_ref, wu_ref, wd_ref, fg_ref, o_ref.at[r], acc_ref.at[r], **kw)
                    for r in tiles], FFN_STAGGER)


def _ffn_tile(x_ref, g_ref, wg_ref, wu_ref, wd_ref, fg_ref, o_ref, acc_ref, *, n_chunks, final_norm):
    xn = _rms(x_ref[...], g_ref[...]).astype(BF16)
    yield
    for j in range(n_chunks):
        cols = slice(j * FFN_COLS, (j + 1) * FFN_COLS)
        gate = _dot(xn, wg_ref[:, cols])
        up = _dot(xn, wu_ref[:, cols])
        h = ((gate * _sigmoid(gate)) * up).astype(BF16)
        d = _dot(h, wd_ref[cols, :])
        if j == 0:
            acc_ref[...] = d
        else:
            acc_ref[...] += d
        yield
    y = x_ref[...] + 0.5 * acc_ref[...]
    if final_norm:
        y = _rms(y, fg_ref[...])
    o_ref[...] = y


def _ffn_cols_kernel(x_ref, g_ref, wg_ref, wu_ref, wd_ref, fg_ref, o_ref, acc_ref, xn_ref, *, final_norm):
    j = pl.program_id(0)

    @pl.when(j == 0)
    def _():
        xn_ref[...] = _rms(x_ref[...], g_ref[...]).astype(BF16)
        acc_ref[...] = jnp.zeros_like(acc_ref)

    xn = xn_ref[...]
    gate = _dot(xn, wg_ref[...])
    up = _dot(xn, wu_ref[...])
    h = ((gate * _sigmoid(gate)) * up).astype(BF16)
    acc_ref[...] += _dot(h, wd_ref[...])

    @pl.when(j == pl.num_programs(0) - 1)
    def _():
        y = x_ref[...] + 0.5 * acc_ref[...]
        if final_norm:
            y = _rms(y, fg_ref[...])
        o_ref[...] = y


def _ffn_cols(x2, g, wg, wu, wd, fg, final_norm):
    n_tok, d = x2.shape
    kern = functools.partial(_ffn_cols_kernel, final_norm=final_norm)
    return pl.pallas_call(
        kern,
        grid=(wg.shape[1] // FFN_COLS,),
        in_specs=[_const_spec(x2.shape), _const_spec(g.shape),
                  pl.BlockSpec((d, FFN_COLS), lambda j: (0, j)),
                  pl.BlockSpec((d, FFN_COLS), lambda j: (0, j)),
                  pl.BlockSpec((FFN_COLS, d), lambda j: (j, 0)),
                  _const_spec(fg.shape)],
        out_specs=pl.BlockSpec((n_tok, d), lambda j: (0, 0)),
        out_shape=jax.ShapeDtypeStruct((n_tok, d), F32),
        scratch_shapes=[pltpu.VMEM((n_tok, d), F32), pltpu.VMEM((n_tok, d), BF16)],
        compiler_params=_cparams(("arbitrary",)),
        name="ffn_cols",
    )(x2, g, wg, wu, wd, fg)


def _ffn(x2, g, wg, wu, wd, fg, final_norm):
    n_tok, d = x2.shape
    if n_tok <= FFN_ROWS:
        return _ffn_cols(x2, g, wg, wu, wd, fg, final_norm)
    rows = min(FFN_ROWS, n_tok)
    n_tiles = FFN_TILES if n_tok % (FFN_TILES * rows) == 0 else 1
    tm = n_tiles * rows
    n_chunks = wg.shape[1] // FFN_COLS
    kern = functools.partial(_ffn_kernel, n_tiles=n_tiles, rows=rows, n_chunks=n_chunks, final_norm=final_norm)
    return pl.pallas_call(
        kern,
        grid=(n_tok // tm,),
        in_specs=[pl.BlockSpec((tm, d), lambda i: (i, 0)),
                  _const_spec(g.shape), _const_spec(wg.shape), _const_spec(wu.shape),
                  _const_spec(wd.shape), _const_spec(fg.shape)],
        out_specs=pl.BlockSpec((tm, d), lambda i: (i, 0)),
        out_shape=jax.ShapeDtypeStruct((n_tok, d), F32),
        scratch_shapes=[pltpu.VMEM((tm, d), F32)],
        compiler_params=_cparams(("parallel",)),
        name="ffn",
    )(x2, g, wg, wu, wd, fg)


N_AB_CONSTS = 9


def _mixer_ab_kernel(x_ref, *refs, n_seq, **dims):
    consts = refs[:N_AB_CONSTS]
    per_seq = refs[N_AB_CONSTS:]
    c0_ref, n0_ref, m0_ref, s0_ref, _, c_ref, n_ref, m_ref, s_ref, st_ref = per_seq[:10]
    li = pl.program_id(1)

    @pl.when(li == 0)
    def _():
        c_ref[...] = c0_ref[...]
        n_ref[...] = n0_ref[...]
        m_ref[...] = m0_ref[...]
        for s in range(n_seq):
            for h in range(HEADS):
                st_ref[s, h] = s0_ref[s, h].T

    _run_staggered([_mixer_ab_one(x_ref.at[s], *consts, *[r.at[s] for r in per_seq], **dims)
                    for s in range(n_seq)], AB_STAGGER)

    @pl.when(li == pl.num_programs(1) - 1)
    def _():
        for s in range(n_seq):
            for h in range(HEADS):
                s_ref[s, h] = st_ref[s, h].T


def _mixer_ab_one(x_ref, ng_ref, wbig_ref, bbig_ref, wgc_ref, bgc_ref,
                  ang_ref, bng_ref, lb_ref, wout_ref, c0_ref, n0_ref, m0_ref, s0_ref,
                  xo_ref, c_ref, n_ref, m_ref, s_ref, st_ref, hmix_ref, *, rows, hchunk):
    T = rows
    W = HEADS * HEAD_DIM
    x = x_ref[...]
    xn = _rms(x, ng_ref[...]).astype(BF16)
    zc = _dot(xn, wgc_ref[...]) + bgc_ref[...]
    yield

    z = []
    for k in range(8):
        halves = []
        for c0 in (k * W, k * W + W // 2):
            halves.append(_dot(xn, wbig_ref[:, c0:c0 + W // 2]) + bbig_ref[:, c0:c0 + W // 2])
            yield
        z.append(jnp.concatenate(halves, axis=1))

    aq = z[0] * (HEAD_DIM ** -0.5)
    ak, av, ao = z[1], z[2], z[3]
    ri = lax.broadcasted_iota(jnp.int32, (T, T), 0)
    ci = lax.broadcasted_iota(jnp.int32, (T, T), 1)
    causal = ri >= ci
    tril = jnp.where(causal, 1.0, 0.0).astype(BF16)
    fc = sum(_dot(tril, p) for p in _split3(_log_sigmoid(zc)))
    zr = zc.T
    fr = fc.T
    yield
    for h in range(HEADS):
        cols = slice(h * HEAD_DIM, (h + 1) * HEAD_DIM)
        ig_c = zc[:, h:h + 1]
        ig_r = zr[h:h + 1, :]
        f_c = fc[:, HEADS + h:HEADS + h + 1]
        f_r = fr[HEADS + h:HEADS + h + 1, :]
        m_prev = m_ref[h:h + 1, 0:1]
        log_src = jnp.where(causal, f_c + (ig_r - f_r), -jnp.inf)
        log_prev = f_c + m_prev
        m_t = jnp.maximum(log_prev, jnp.max(log_src, axis=-1, keepdims=True))
        w_src = jnp.exp(log_src - m_t)
        w_prev = jnp.exp(log_prev - m_t)
        qf = aq[:, cols]
        kf = ak[:, cols]
        q = qf.astype(BF16)
        v = av[:, cols].astype(BF16)
        s = _dot_nt(q, kf.astype(BF16)) * w_src
        yield
        c_old = c_ref[h]
        n_old = n_ref[h:h + 1, :]
        num = _dot(s.astype(BF16), v) + w_prev * _dot(q, c_old.astype(BF16))
        den = jnp.sum(s, axis=-1, keepdims=True) + w_prev * jnp.sum(qf * n_old, axis=-1, keepdims=True)
        den = jnp.maximum(jnp.abs(den), jnp.exp(-m_t))
        hh = num / den
        f_last = f_c[T - 1:T, :]
        m_new = m_t[T - 1:T, :]
        decay = jnp.exp(f_last + m_prev - m_new)
        w_end = jnp.exp(f_last - f_c + ig_c - m_new)
        wk = w_end * kf
        c_ref[h] = decay * c_old + _dot_tn(wk.astype(BF16), v)
        n_ref[h:h + 1, :] = decay * n_old + jnp.sum(wk, axis=0, keepdims=True)
        m_ref[h:h + 1, :] = jnp.broadcast_to(m_new, (1, N_GATE_LANES))
        y = hh * lax.rsqrt(jnp.mean(hh * hh, axis=-1, keepdims=True) + EPS)
        out = (y * ang_ref[:, cols]) * _sigmoid(ao[:, cols])
        hmix_ref[:, cols] = out.astype(BF16)
        yield

    qh = z[4] * _sigmoid(z[4])
    zf, iv, bg = z[5], z[6], z[7]
    lb = lb_ref[...]
    lf = jnp.log(lb + (1.0 - lb) * _sigmoid(zf))
    kk = (1.0 - lb) * _sigmoid(-zf)
    sh = int(math.log2(hchunk))
    same_chunk = lax.shift_right_logical(ri, sh) == lax.shift_right_logical(ci, sh)
    blocktril = jnp.where(causal & same_chunk, 1.0, 0.0).astype(BF16)
    G = sum(_dot(blocktril, p) for p in _split3(lf))
    q_dec = (qh * jnp.exp(G)).astype(BF16)
    k_inv = (kk * jnp.exp(-G)).astype(BF16)
    ivb = iv.astype(BF16)
    yield
    rs = lax.broadcasted_iota(jnp.int32, (hchunk, hchunk), 0)
    cs = lax.broadcasted_iota(jnp.int32, (hchunk, hchunk), 1)
    sub_causal = rs >= cs
    for j in range(T // hchunk):
        r = slice(j * hchunk, (j + 1) * hchunk)
        g_end = G[(j + 1) * hchunk - 1:(j + 1) * hchunk, :]
        k_end = (kk[r, :] * jnp.exp(g_end - G[r, :])).astype(BF16)
        e_end = jnp.exp(g_end)
        for h in range(HEADS):
            cols = slice(h * HEAD_DIM, (h + 1) * HEAD_DIM)
            qd = q_dec[r, cols]
            att = jnp.where(sub_causal, _dot_nt(qd, k_inv[r, cols]), 0.0)
            st_old = st_ref[h]
            o = _dot(att.astype(BF16), ivb[r, cols]) + _dot_nt(qd, st_old.astype(BF16))
            st_ref[h] = st_old * e_end[:, cols] + _dot_tn(ivb[r, cols], k_end[:, cols])
            y = o * lax.rsqrt(jnp.mean(o * o, axis=-1, keepdims=True) + EPS)
            gate = bg[r, cols]
            out = (y * bng_ref[:, cols]) * (gate * _sigmoid(gate))
            hmix_ref[r, W + h * HEAD_DIM:W + (h + 1) * HEAD_DIM] = out.astype(BF16)
            if h % 2 == 1:
                yield

    d_model = x.shape[-1]
    for k in range(d_model // OUT_COLS):
        cols = slice(k * OUT_COLS, (k + 1) * OUT_COLS)
        xo_ref[:, cols] = x[:, cols] + _dot(hmix_ref[...], wout_ref[:, cols])
        yield


def _mixer_ab(x, ng, w, c0, n0, m0, s0):
    B, L, D = x.shape
    T = min(MIX_ROWS, L)
    hchunk = HGRN_CHUNK if T % HGRN_CHUNK == 0 else T
    W = HEADS * HEAD_DIM
    nb = _seqs_per_step(B, T)
    kern = functools.partial(_mixer_ab_kernel, n_seq=nb, rows=T, hchunk=hchunk)
    state4 = pl.BlockSpec((nb, HEADS, HEAD_DIM, HEAD_DIM), lambda b, l: (b, 0, 0, 0))
    state_n = pl.BlockSpec((nb, HEADS, HEAD_DIM), lambda b, l: (b, 0, 0))
    state_m = pl.BlockSpec((nb, 8, N_GATE_LANES), lambda b, l: (b, 0, 0))
    xspec = pl.BlockSpec((nb, T, D), lambda b, l: (b, l, 0))
    consts = (ng, w["w_big"], w["b_big"], w["w_gc"], w["b_gc"],
              w["a_norm_g"], w["b_norm_g"], w["lb"], w["w_out"])
    assert len(consts) == N_AB_CONSTS
    return pl.pallas_call(
        kern,
        grid=(B // nb, L // T),
        in_specs=[xspec] + [_const_spec(c.shape) for c in consts] + [state4, state_n, state_m, state4],
        out_specs=[xspec, state4, state_n, state_m, state4],
        out_shape=[jax.ShapeDtypeStruct((B, L, D), F32),
                   jax.ShapeDtypeStruct((B, HEADS, HEAD_DIM, HEAD_DIM), F32),
                   jax.ShapeDtypeStruct((B, HEADS, HEAD_DIM), F32),
                   jax.ShapeDtypeStruct((B, 8, N_GATE_LANES), F32),
                   jax.ShapeDtypeStruct((B, HEADS, HEAD_DIM, HEAD_DIM), F32)],
        scratch_shapes=[pltpu.VMEM((nb, HEADS, HEAD_DIM, HEAD_DIM), F32),
                        pltpu.VMEM((nb, T, 2 * W), BF16)],
        compiler_params=_cparams(("parallel", "arbitrary")),
        name="mixer_ab",
    )(x, *consts, c0, n0, m0, s0)


N_CD_CONSTS = 24


def _mixer_cd_kernel(x_ref, *refs, n_seq, **dims):
    consts = refs[:N_CD_CONSTS]
    per_seq = refs[N_CD_CONSTS:]
    h0_ref, conv0_ref, sre0_ref, sim0_ref, _, h_ref, conv_ref, sre_ref, sim_ref = per_seq[:9]

    @pl.when(pl.program_id(1) == 0)
    def _():
        h_ref[...] = h0_ref[...]
        conv_ref[...] = conv0_ref[...]
        sre_ref[...] = sre0_ref[...]
        sim_ref[...] = sim0_ref[...]

    _run_staggered([_mixer_cd_one(x_ref.at[s], *consts, *[r.at[s] for r in per_seq], **dims)
                    for s in range(n_seq)], CD_PHASE_PIECES)


def _mixer_cd_one(x_ref, ng_ref, perm_ref, permt_ref, wcd_ref, bcd_ref, cw_ref, cb_ref, wa_ref, ba_ref,
                  wx_ref, bx_ref, lam_ref, wbu_ref, abr_ref, abi_ref, pr_ref, pi_ref, pwr_ref, pwi_ref,
                  wc_ref, dsk_ref, wglu_ref, bglu_ref, wout_ref, h0_ref, conv0_ref, sre0_ref, sim0_ref,
                  xo_ref, h_ref, conv_ref, sre_ref, sim_ref, ext_ref, lre_ref, lim_ref, sb_ref, hmix_ref,
                  zg_ref, zd_ref, *, rows, c_width, d_width, n_state, gate_tile):
    T = rows
    n = T // 8
    log_n = int(math.log2(n))
    halo = (CONV_W - 1) * 8
    xn = _rms(x_ref[...], ng_ref[...]).astype(BF16)
    xp = _dot(perm_ref[...], xn).astype(BF16)
    yield

    for k in range((2 * c_width + d_width) // gate_tile):
        lo = k * gate_tile
        tile = _dot(xp, wcd_ref[:, lo:lo + gate_tile]) + bcd_ref[:, lo:lo + gate_tile]
        if lo < c_width:
            zg_ref[:, lo:lo + gate_tile] = tile
        elif lo < 2 * c_width:
            ext_ref[halo:halo + T, lo - c_width:lo - c_width + gate_tile] = tile
        else:
            zd_ref[:, lo - 2 * c_width:lo - 2 * c_width + gate_tile] = tile
        yield

    hist = conv_ref[...]
    sub_c = lax.broadcasted_iota(jnp.int32, (8, c_width), 0)
    for k in range(CONV_W - 1):
        g = n - (CONV_W - 1) + k
        grp = ext_ref[halo + g * 8:halo + (g + 1) * 8, :]
        ext_ref[k * 8:(k + 1) * 8, :] = jnp.where(sub_c == 0, hist[k:k + 1, :], pltpu.roll(grp, 1, 0))
        conv_ref[k:k + 1, :] = grp[7:8, :]
    u = cb_ref[...] + cw_ref[CONV_W - 1:CONV_W, :] * ext_ref[halo:halo + T, :]
    for k in range(CONV_W - 1):
        u = u + cw_ref[k:k + 1, :] * ext_ref[k * 8:k * 8 + T, :]
    yield

    ub = u.astype(BF16)
    log_sig_lam = _log_sigmoid(lam_ref[...])
    sub_t = lax.broadcasted_iota(jnp.int32, (8, gate_tile), 0)
    for k in range(c_width // gate_tile):
        cols = slice(k * gate_tile, (k + 1) * gate_tile)
        r = _sigmoid(_dot(ub[:, cols], wa_ref[k]) + ba_ref[:, cols])
        i = _sigmoid(_dot(ub[:, cols], wx_ref[k]) + bx_ref[:, cols])
        log_a = (RG_C * r) * log_sig_lam[:, cols]
        a = jnp.exp(log_a)
        b = jnp.sqrt(-jnp.tanh(log_a) * (a * a + 1.0)) * (i * u[:, cols])
        dec = a[0:8, :]
        loc = b[0:8, :]
        decs, locs = [dec], [loc]
        for j in range(1, n):
            aj = a[j * 8:(j + 1) * 8, :]
            loc = aj * loc + b[j * 8:(j + 1) * 8, :]
            dec = aj * dec
            decs.append(dec)
            locs.append(loc)
        h_prev = h_ref[:, cols]
        ea = dec
        eb = loc + jnp.where(sub_t == 0, dec * h_prev, 0.0)
        for d in (1, 2, 4):
            a_sh = jnp.where(sub_t >= d, pltpu.roll(ea, d, 0), 1.0)
            b_sh = jnp.where(sub_t >= d, pltpu.roll(eb, d, 0), 0.0)
            eb = ea * b_sh + eb
            ea = ea * a_sh
        carry = jnp.where(sub_t == 0, h_prev, pltpu.roll(eb, 1, 0))
        h_ref[:, cols] = eb[7:8, :]
        h = jnp.concatenate([locs[j] + decs[j] * carry for j in range(n)], axis=0)
        hmix_ref[:, cols] = (jax.nn.gelu(zg_ref[:, cols], approximate=True) * h).astype(BF16)
        yield

    bu = _dot(zd_ref[...].astype(BF16), wbu_ref[...])
    for lo in range(0, n_state, S5_LANES):
        re_cols = slice(lo, lo + S5_LANES)
        im_cols = slice(n_state + lo, n_state + lo + S5_LANES)
        p_re = jnp.broadcast_to(abr_ref[:, re_cols], (8, S5_LANES))
        p_im = jnp.broadcast_to(abi_ref[:, re_cols], (8, S5_LANES))
        l_re = bu[0:8, re_cols]
        l_im = bu[0:8, im_cols]
        lre_ref[0:8, re_cols] = l_re
        lim_ref[0:8, re_cols] = l_im
        for j in range(1, n):
            rows_j = slice(j * 8, (j + 1) * 8)
            l_re, l_im = (bu[rows_j, re_cols] + (p_re * l_re - p_im * l_im),
                          bu[rows_j, im_cols] + (p_re * l_im + p_im * l_re))
            lre_ref[rows_j, re_cols] = l_re
            lim_ref[rows_j, re_cols] = l_im
        yield
        sub_s = lax.broadcasted_iota(jnp.int32, (8, S5_LANES), 0)
        s_re = sre_ref[:, re_cols]
        s_im = sim_ref[:, re_cols]
        q_re = pr_ref[log_n:log_n + 1, re_cols]
        q_im = pi_ref[log_n:log_n + 1, re_cols]
        e_re = l_re + jnp.where(sub_s == 0, q_re * s_re - q_im * s_im, 0.0)
        e_im = l_im + jnp.where(sub_s == 0, q_re * s_im + q_im * s_re, 0.0)
        for k, d in enumerate((1, 2, 4)):
            q_re = pr_ref[log_n + k:log_n + k + 1, re_cols]
            q_im = pi_ref[log_n + k:log_n + k + 1, re_cols]
            r_sh = jnp.where(sub_s >= d, pltpu.roll(e_re, d, 0), 0.0)
            i_sh = jnp.where(sub_s >= d, pltpu.roll(e_im, d, 0), 0.0)
            e_re, e_im = e_re + (q_re * r_sh - q_im * i_sh), e_im + (q_re * i_sh + q_im * r_sh)
        c_re = jnp.where(sub_s == 0, s_re, pltpu.roll(e_re, 1, 0))
        c_im = jnp.where(sub_s == 0, s_im, pltpu.roll(e_im, 1, 0))
        sre_ref[:, re_cols] = e_re[7:8, :]
        sim_ref[:, re_cols] = e_im[7:8, :]
        for j in range(0, n, 2):
            f_re, f_im = [], []
            for jj in (j, j + 1):
                rows_j = slice(jj * 8, (jj + 1) * 8)
                w_re = pwr_ref[rows_j, re_cols]
                w_im = pwi_ref[rows_j, re_cols]
                f_re.append(lre_ref[rows_j, re_cols] + (w_re * c_re - w_im * c_im))
                f_im.append(lim_ref[rows_j, re_cols] + (w_re * c_im + w_im * c_re))
            sb_ref[j * 8:(j + 2) * 8, re_cols] = jnp.concatenate(f_re, axis=0).astype(BF16)
            sb_ref[j * 8:(j + 2) * 8, im_cols] = jnp.concatenate(f_im, axis=0).astype(BF16)
        yield
    y = _dot(sb_ref[...], wc_ref[...]) + dsk_ref[...] * zd_ref[...]
    y = jax.nn.gelu(y, approximate=True)
    y = y * _sigmoid(_dot(y.astype(BF16), wglu_ref[...]) + bglu_ref[...])
    hmix_ref[:, c_width:c_width + d_width] = y.astype(BF16)
    yield

    d_model = x_ref.shape[-1]
    mixed = []
    for k in range(d_model // gate_tile):
        cols = slice(k * gate_tile, (k + 1) * gate_tile)
        mixed.append(_dot(permt_ref[...], hmix_ref[:, cols]).astype(BF16))
        yield
    mixed = jnp.concatenate(mixed, axis=1)
    for k in range(d_model // gate_tile):
        cols = slice(k * gate_tile, (k + 1) * gate_tile)
        xo_ref[:, cols] = x_ref[:, cols] + _dot(mixed, wout_ref[:, cols])
        yield


def _mixer_cd(x, ng, w, h0, conv0, sre0, sim0):
    B, L, D = x.shape
    T = min(MIX_ROWS, L)
    c_width = h0.shape[-1]
    n_state = sre0.shape[-1]
    d_width = w["w_glu"].shape[0]
    nb = _seqs_per_step(B, T)
    kern = functools.partial(_mixer_cd_kernel, n_seq=nb, rows=T, c_width=c_width, d_width=d_width,
                             n_state=n_state, gate_tile=w["w_a"].shape[-1])
    xspec = pl.BlockSpec((nb, T, D), lambda b, l: (b, l, 0))
    st_h = pl.BlockSpec((nb, 1, c_width), lambda b, l: (b, 0, 0))
    st_conv = pl.BlockSpec((nb, CONV_W - 1, c_width), lambda b, l: (b, 0, 0))
    st_s = pl.BlockSpec((nb, 1, n_state), lambda b, l: (b, 0, 0))
    perm, perm_t = _regroup_matrices(T)
    n_pow = (T // 8) * 8
    consts = (ng, perm, perm_t, w["w_cd"], w["b_cd"], w["conv_w"], w["conv_b"], w["w_a"], w["b_a"], w["w_x"],
              w["b_x"], w["lam"], w["w_bu"], w["ab_re"], w["ab_im"], w["pow_re"], w["pow_im"],
              w["run_re"][:n_pow], w["run_im"][:n_pow], w["w_c"], w["d_skip"], w["w_glu"], w["b_glu"], w["w_out"])
    assert len(consts) == N_CD_CONSTS
    return pl.pallas_call(
        kern,
        grid=(B // nb, L // T),
        in_specs=[xspec] + [_const_spec(c.shape) for c in consts] + [st_h, st_conv, st_s, st_s],
        out_specs=[xspec, st_h, st_conv, st_s, st_s],
        out_shape=[jax.ShapeDtypeStruct((B, L, D), F32),
                   jax.ShapeDtypeStruct((B, 1, c_width), F32),
                   jax.ShapeDtypeStruct((B, CONV_W - 1, c_width), F32),
                   jax.ShapeDtypeStruct((B, 1, n_state), F32),
                   jax.ShapeDtypeStruct((B, 1, n_state), F32)],
        scratch_shapes=[pltpu.VMEM((nb, T + (CONV_W - 1) * 8, c_width), F32),
                        pltpu.VMEM((nb, T, n_state), F32),
                        pltpu.VMEM((nb, T, n_state), F32),
                        pltpu.VMEM((nb, T, 2 * n_state), BF16),
                        pltpu.VMEM((nb, T, c_width + d_width), BF16),
                        pltpu.VMEM((nb, T, c_width), F32),
                        pltpu.VMEM((nb, T, d_width), F32)],
        compiler_params=_cparams(("parallel", "arbitrary")),
        name="mixer_cd",
    )(x, *consts, h0, conv0, sre0, sim0)


def _regroup_matrices(rows):
    n = rows // 8
    r = np.arange(rows)
    p = np.zeros((rows, rows), np.float32)
    p[r, (r % 8) * n + r // 8] = 1.0
    return jnp.asarray(p, BF16), jnp.asarray(p.T, BF16)


def _prep_s5_kernel(are_ref, aim_ref, ldt_ref, wbre_ref, wbim_ref, abr_ref, abi_ref, pr_ref, pi_ref,
                    runr_ref, runi_ref, wbu_ref, *, n_state):
    a_re = are_ref[...]
    a_im = aim_ref[...]
    dt = jnp.exp(ldt_ref[...])
    mag = jnp.exp(dt * a_re)
    ab_re = mag * jnp.cos(dt * a_im)
    ab_im = mag * jnp.sin(dt * a_im)
    inv = 1.0 / (a_re * a_re + a_im * a_im)
    z_re = ((ab_re - 1.0) * a_re + ab_im * a_im) * inv
    z_im = (ab_im * a_re - (ab_re - 1.0) * a_im) * inv
    abr_ref[...] = ab_re
    abi_ref[...] = ab_im
    wb_re = wbre_ref[...]
    wb_im = wbim_ref[...]
    wbu_ref[:, :n_state] = (z_re * wb_re - z_im * wb_im).astype(BF16)
    wbu_ref[:, n_state:] = (z_re * wb_im + z_im * wb_re).astype(BF16)
    p_re, p_im = ab_re, ab_im
    for k in range(MAX_SCAN_STEPS):
        pr_ref[k:k + 1, :] = p_re
        pi_ref[k:k + 1, :] = p_im
        p_re, p_im = p_re * p_re - p_im * p_im, 2.0 * (p_re * p_im)
    p_re, p_im = ab_re, ab_im
    for j in range(MIX_ROWS // 8):
        runr_ref[j * 8:(j + 1) * 8, :] = jnp.broadcast_to(p_re, (8, n_state))
        runi_ref[j * 8:(j + 1) * 8, :] = jnp.broadcast_to(p_im, (8, n_state))
        p_re, p_im = p_re * ab_re - p_im * ab_im, p_re * ab_im + p_im * ab_re


def _prep_s5(a_re, a_im, log_dt_rep, wb_re, wb_im):
    n_state = a_re.shape[-1]
    d_width = wb_re.shape[0]
    kern = functools.partial(_prep_s5_kernel, n_state=n_state)
    row = jax.ShapeDtypeStruct((1, n_state), F32)
    tab = jax.ShapeDtypeStruct((MAX_SCAN_STEPS, n_state), F32)
    run = jax.ShapeDtypeStruct((MIX_ROWS, n_state), F32)
    return pl.pallas_call(
        kern,
        out_shape=[row, row, tab, tab, run, run, jax.ShapeDtypeStruct((d_width, 2 * n_state), BF16)],
        name="prep_s5",
    )(a_re, a_im, log_dt_rep, wb_re, wb_im)


def _prep_lb_kernel(logits_ref, lb_ref):
    lg = logits_ref[...]
    n = lg.shape[0]
    mx = lg[0:1, :]
    for i in range(1, n):
        mx = jnp.maximum(mx, lg[i:i + 1, :])
    e = [jnp.exp(lg[i:i + 1, :] - mx) for i in range(n)]
    tot = e[0]
    for i in range(1, n):
        tot = tot + e[i]
    run = None
    for i in range(n):
        p = e[i] / tot
        run = p if run is None else run + p
        lb_ref[i:i + 1, :] = run


def _prep_lb(logits):
    return pl.pallas_call(_prep_lb_kernel, out_shape=jax.ShapeDtypeStruct(logits.shape, F32),
                          name="prep_lb")(logits)


def _block_diag(blocks):
    n, r, c = blocks.shape
    eye = jnp.eye(n, dtype=blocks.dtype)
    return (eye[:, None, :, None] * blocks[:, :, None, :]).reshape(n * r, n * c)


def kernel(x_prompt, x_sample, state_mlstm_C, state_mlstm_n, state_mlstm_m, state_hgrn_S, state_rglru_h,
           state_rglru_conv, state_s5_re, state_s5_im, norm_g, final_norm_g, ffn_w_gate, ffn_w_up, ffn_w_down,
           ab_w_in, ab_b_in, mlstm_norm_g, hgrn_norm_g, hgrn_lb_logits, ab_w_out, cd_w_in, cd_b_in, conv_w,
           conv_b, rg_w_a, rg_b_a, rg_w_x, rg_b_x, rg_lambda, s5_A_re, s5_A_im, s5_log_dt, s5_B_re, s5_B_im,
           s5_C_re, s5_C_im, s5_D, s5_w_glu, s5_b_glu, cd_w_out):
    depth = norm_g.shape[0]
    d_model = x_prompt.shape[-1]
    W = HEADS * HEAD_DIM
    c_width = rg_lambda.shape[-1]
    groups, n_st, s5_ch = s5_B_re.shape[1:]
    n_state = groups * n_st
    d_width = groups * s5_ch
    gate_tile = 256
    blocks_per_tile = gate_tile // rg_w_a.shape[-1]

    lb_all = _prep_lb(hgrn_lb_logits.astype(F32))

    ffn = []
    for l in range(depth):
        ffn.append([(ffn_w_gate[l, i].astype(BF16), ffn_w_up[l, i].astype(BF16), ffn_w_down[l, i].astype(BF16))
                    for i in range(2)])
    mix_w = []
    for l in range(depth):
        j = l // 2
        if l % 2 == 0:
            wi, bi = ab_w_in[j], ab_b_in[j]
            big = jnp.concatenate([wi[:, :4 * W], wi[:, 4 * W + 2 * HEADS:]], axis=1)
            b_big = jnp.concatenate([bi[:4 * W], bi[4 * W + 2 * HEADS:]])[None, :]
            wg = wi[:, 4 * W:4 * W + 2 * HEADS]
            bgate = bi[4 * W:4 * W + 2 * HEADS]
            mix_w.append(dict(
                w_big=big.astype(BF16), b_big=b_big,
                w_gc=jnp.pad(wg, ((0, 0), (0, N_GATE_LANES - 2 * HEADS))).astype(BF16),
                b_gc=jnp.pad(bgate, (0, N_GATE_LANES - 2 * HEADS))[None, :],
                a_norm_g=mlstm_norm_g[j][None, :], b_norm_g=hgrn_norm_g[j][None, :],
                lb=lb_all[l][None, :], w_out=ab_w_out[j].astype(BF16)))
        else:
            n_tiles = c_width // gate_tile

            def tiles(wb):
                wb = wb.reshape(n_tiles, blocks_per_tile, wb.shape[-2], wb.shape[-1])
                return jnp.stack([_block_diag(wb[t]) for t in range(n_tiles)]).astype(BF16)

            wb_re = _block_diag(jnp.swapaxes(s5_B_re[j], 1, 2))
            wb_im = _block_diag(jnp.swapaxes(s5_B_im[j], 1, 2))
            ab_re, ab_im, pow_re, pow_im, run_re, run_im, w_bu = _prep_s5(
                s5_A_re[j].reshape(1, n_state), s5_A_im[j].reshape(1, n_state),
                jnp.repeat(s5_log_dt[j], n_st)[None, :], wb_re, wb_im)
            w_c = jnp.concatenate([_block_diag(jnp.swapaxes(s5_C_re[j], 1, 2)),
                                   -_block_diag(jnp.swapaxes(s5_C_im[j], 1, 2))], axis=0)
            mix_w.append(dict(
                w_cd=cd_w_in[j].astype(BF16), b_cd=cd_b_in[j][None, :], conv_w=conv_w[j], conv_b=conv_b[j][None, :],
                w_a=tiles(rg_w_a[j]), b_a=rg_b_a[j][None, :], w_x=tiles(rg_w_x[j]), b_x=rg_b_x[j][None, :],
                lam=rg_lambda[j][None, :], w_bu=w_bu, ab_re=ab_re, ab_im=ab_im, pow_re=pow_re, pow_im=pow_im,
                run_re=run_re, run_im=run_im,
                w_c=w_c.astype(BF16), d_skip=s5_D[j][None, :], w_glu=s5_w_glu[j].astype(BF16),
                b_glu=s5_b_glu[j][None, :], w_out=cd_w_out[j].astype(BF16)))
    fg = final_norm_g[None, :]

    def run_trunk(x, states):
        mC, mn, mm, hS, rh, rc, sre, sim = states
        B, L, _ = x.shape
        outs = ([], [], [], [], [], [], [], [])
        for l in range(depth):
            j = l // 2
            wg, wu, wd = ffn[l][0]
            x = _ffn(x.reshape(B * L, d_model), norm_g[l, 0][None, :], wg, wu, wd, fg, False).reshape(B, L, d_model)
            if l % 2 == 0:
                m0 = jnp.broadcast_to(jnp.pad(mm[j], ((0, 0), (0, 8 - HEADS)))[:, :, None], (B, 8, N_GATE_LANES))
                x, c1, n1, m1, s1 = _mixer_ab(x, norm_g[l, 1][None, :], mix_w[l], mC[j], mn[j], m0, hS[j])
                for lst, val in zip(outs[:4], (c1, n1, m1[:, :HEADS, 0], s1)):
                    lst.append(val)
            else:
                x, h1, buf1, re1, im1 = _mixer_cd(x, norm_g[l, 1][None, :], mix_w[l], rh[j][:, None, :], rc[j],
                                                  sre[j].reshape(B, 1, n_state), sim[j].reshape(B, 1, n_state))
                for lst, val in zip(outs[4:], (h1[:, 0], buf1, re1.reshape(B, groups, n_st),
                                               im1.reshape(B, groups, n_st))):
                    lst.append(val)
            wg, wu, wd = ffn[l][1]
            x = _ffn(x.reshape(B * L, d_model), norm_g[l, 2][None, :], wg, wu, wd, fg,
                     l == depth - 1).reshape(B, L, d_model)
        return x, [jnp.stack(lst) for lst in outs]

    n_even = (depth + 1) // 2
    n_odd = depth // 2
    bp = x_prompt.shape[0]
    zero_states = (jnp.zeros((n_even, bp, HEADS, HEAD_DIM, HEAD_DIM), F32),
                   jnp.zeros((n_even, bp, HEADS, HEAD_DIM), F32),
                   jnp.zeros((n_even, bp, HEADS), F32),
                   jnp.zeros((n_even, bp, HEADS, HEAD_DIM, HEAD_DIM), F32),
                   jnp.zeros((n_odd, bp, c_width), F32),
                   jnp.zeros((n_odd, bp, CONV_W - 1, c_width), F32),
                   jnp.zeros((n_odd, bp, groups, n_st), F32),
                   jnp.zeros((n_odd, bp, groups, n_st), F32))
    y_prompt, p_st = run_trunk(x_prompt, zero_states)
    y_sample, s_st = run_trunk(x_sample, (state_mlstm_C, state_mlstm_n, state_mlstm_m, state_hgrn_S,
                                          state_rglru_h, state_rglru_conv, state_s5_re, state_s5_im))
    return (y_prompt, y_sample, *p_st, *s_st)
```

```python
import functools
import math

import jax
import jax.numpy as jnp
import numpy as np
from jax import lax
from jax.experimental import pallas as pl
from jax.experimental.pallas import tpu as pltpu

F32 = jnp.float32
BF16 = jnp.bfloat16

EPS = 1e-6
RG_C = 8.0
CONV_W = 4
HEADS = 4
HEAD_DIM = 128
N_GATE_LANES = 128
V7X_VMEM_LIMIT = 56 * 1024 * 1024
FFN_COLS = 256
FFN_ROWS = 512
FFN_TILES = 2
FFN_STAGGER = 6
MIX_ROWS = 256
HGRN_CHUNK = 64
MAX_SCAN_STEPS = 8
MIX_STEP_ROWS = 1024
MAX_STEP_SEQS = 4
AB_STAGGER = 9
OUT_COLS = 256
S5_LANES = 1024
CD_PHASE_PIECES = 8


def _dot(a, b):
    return jnp.dot(a, b, preferred_element_type=F32)


def _dot_nt(a, b):
    return lax.dot_general(a, b, (((1,), (1,)), ((), ())), preferred_element_type=F32)


def _dot_tn(a, b):
    return lax.dot_general(a, b, (((0,), (0,)), ((), ())), preferred_element_type=F32)


def _rms(x, g):
    y = x * lax.rsqrt(jnp.mean(x * x, axis=-1, keepdims=True) + EPS)
    return y * g


def _sigmoid(x):
    return jax.nn.sigmoid(x)


def _log_sigmoid(x):
    return jnp.minimum(x, 0.0) - jnp.log1p(jnp.exp(-jnp.abs(x)))


def _split3(a):
    hi = a.astype(BF16)
    r = a - hi.astype(F32)
    mid = r.astype(BF16)
    lo = (r - mid.astype(F32)).astype(BF16)
    return hi, mid, lo


def _const_spec(shape):
    nd = len(shape)
    return pl.BlockSpec(shape, lambda *_: (0,) * nd, pipeline_mode=pl.Buffered(1))


def _run_staggered(piece_generators, offset):
    pending = list(piece_generators)
    live = []
    tick = 0
    while pending or live:
        if pending and tick % offset == 0:
            live.append(pending.pop(0))
        for g in list(live):
            try:
                next(g)
            except StopIteration:
                live.remove(g)
        tick += 1


def _seqs_per_step(n_seqs, rows):
    want = max(1, min(MAX_STEP_SEQS, MIX_STEP_ROWS // rows))
    return max(d for d in range(1, n_seqs + 1) if n_seqs % d == 0 and d <= want)


def _stagger(rows, full_block_stagger):
    return full_block_stagger if rows >= MIX_ROWS else 1


def _cparams(sem):
    return pltpu.CompilerParams(dimension_semantics=sem, vmem_limit_bytes=V7X_VMEM_LIMIT)


def _ffn_kernel(x_ref, g_ref, wg_ref, wu_ref, wd_ref, fg_ref, o_ref, acc_ref, *, n_tiles, rows, **kw):
    tiles = [slice(t * rows, (t + 1) * rows) for t in range(n_tiles)]
    _run_staggered([_ffn_tile(x_ref.at[r], g_ref, wg_ref, wu_ref, wd_ref, fg_ref, o_ref.at[r], acc_ref.at[r], **kw)
                    for r in tiles], FFN_STAGGER)


def _ffn_tile(x_ref, g_ref, wg_ref, wu_ref, wd_ref, fg_ref, o_ref, acc_ref, *, n_chunks, final_norm):
    xn = _rms(x_ref[...], g_ref[...]).astype(BF16)
    yield
    for j in range(n_chunks):
        cols = slice(j * FFN_COLS, (j + 1) * FFN_COLS)
        gate = _dot(xn, wg_ref[:, cols])
        up = _dot(xn, wu_ref[:, cols])
        h = ((gate * _sigmoid(gate)) * up).astype(BF16)
        d = _dot(h, wd_ref[cols, :])
        if j == 0:
            acc_ref[...] = d
        else:
            acc_ref[...] += d
        yield
    y = x_ref[...] + 0.5 * acc_ref[...]
    if final_norm:
        y = _rms(y, fg_ref[...])
    o_ref[...] = y


def _ffn(x2, g, wg, wu, wd, fg, final_norm):
    n_tok, d = x2.shape
    rows = min(FFN_ROWS, n_tok)
    n_tiles = FFN_TILES if n_tok % (FFN_TILES * rows) == 0 else 1
    tm = n_tiles * rows
    n_chunks = wg.shape[1] // FFN_COLS
    kern = functools.partial(_ffn_kernel, n_tiles=n_tiles, rows=rows, n_chunks=n_chunks, final_norm=final_norm)
    return pl.pallas_call(
        kern,
        grid=(n_tok // tm,),
        in_specs=[pl.BlockSpec((tm, d), lambda i: (i, 0)),
                  _const_spec(g.shape), _const_spec(wg.shape), _const_spec(wu.shape),
                  _const_spec(wd.shape), _const_spec(fg.shape)],
        out_specs=pl.BlockSpec((tm, d), lambda i: (i, 0)),
        out_shape=jax.ShapeDtypeStruct((n_tok, d), F32),
        scratch_shapes=[pltpu.VMEM((tm, d), F32)],
        compiler_params=_cparams(("parallel",)),
        name="ffn",
    )(x2, g, wg, wu, wd, fg)


N_AB_CONSTS = 9


def _mixer_ab_kernel(x_ref, *refs, n_seq, stagger, **dims):
    consts = refs[:N_AB_CONSTS]
    per_seq = refs[N_AB_CONSTS:]
    c0_ref, n0_ref, m0_ref, s0_ref, _, c_ref, n_ref, m_ref, s_ref, st_ref = per_seq[:10]
    li = pl.program_id(1)

    @pl.when(li == 0)
    def _():
        c_ref[...] = c0_ref[...]
        n_ref[...] = n0_ref[...]
        m_ref[...] = m0_ref[...]
        for s in range(n_seq):
            for h in range(HEADS):
                st_ref[s, h] = s0_ref[s, h].T

    _run_staggered([_mixer_ab_one(x_ref.at[s], *consts, *[r.at[s] for r in per_seq], **dims)
                    for s in range(n_seq)], stagger)

    @pl.when(li == pl.num_programs(1) - 1)
    def _():
        for s in range(n_seq):
            for h in range(HEADS):
                s_ref[s, h] = st_ref[s, h].T


def _mixer_ab_one(x_ref, ng_ref, wbig_ref, bbig_ref, wgc_ref, bgc_ref,
                  ang_ref, bng_ref, lb_ref, wout_ref, c0_ref, n0_ref, m0_ref, s0_ref,
                  xo_ref, c_ref, n_ref, m_ref, s_ref, st_ref, hmix_ref, *, rows, hchunk):
    T = rows
    W = HEADS * HEAD_DIM
    x = x_ref[...]
    xn = _rms(x, ng_ref[...]).astype(BF16)
    zc = _dot(xn, wgc_ref[...]) + bgc_ref[...]
    yield

    z = []
    for k in range(8):
        halves = []
        for c0 in (k * W, k * W + W // 2):
            halves.append(_dot(xn, wbig_ref[:, c0:c0 + W // 2]) + bbig_ref[:, c0:c0 + W // 2])
            yield
        z.append(jnp.concatenate(halves, axis=1))

    aq = z[0] * (HEAD_DIM ** -0.5)
    ak, av, ao = z[1], z[2], z[3]
    ri = lax.broadcasted_iota(jnp.int32, (T, T), 0)
    ci = lax.broadcasted_iota(jnp.int32, (T, T), 1)
    causal = ri >= ci
    tril = jnp.where(causal, 1.0, 0.0).astype(BF16)
    fc = sum(_dot(tril, p) for p in _split3(_log_sigmoid(zc)))
    zr = zc.T
    fr = fc.T
    yield
    for h in range(HEADS):
        cols = slice(h * HEAD_DIM, (h + 1) * HEAD_DIM)
        ig_c = zc[:, h:h + 1]
        ig_r = zr[h:h + 1, :]
        f_c = fc[:, HEADS + h:HEADS + h + 1]
        f_r = fr[HEADS + h:HEADS + h + 1, :]
        m_prev = m_ref[h:h + 1, 0:1]
        log_src = jnp.where(causal, f_c + (ig_r - f_r), -jnp.inf)
        log_prev = f_c + m_prev
        m_t = jnp.maximum(log_prev, jnp.max(log_src, axis=-1, keepdims=True))
        w_src = jnp.exp(log_src - m_t)
        w_prev = jnp.exp(log_prev - m_t)
        qf = aq[:, cols]
        kf = ak[:, cols]
        q = qf.astype(BF16)
        v = av[:, cols].astype(BF16)
        s = _dot_nt(q, kf.astype(BF16)) * w_src
        yield
        c_old = c_ref[h]
        n_old = n_ref[h:h + 1, :]
        num = _dot(s.astype(BF16), v) + w_prev * _dot(q, c_old.astype(BF16))
        den = jnp.sum(s, axis=-1, keepdims=True) + w_prev * jnp.sum(qf * n_old, axis=-1, keepdims=True)
        den = jnp.maximum(jnp.abs(den), jnp.exp(-m_t))
        hh = num / den
        f_last = f_c[T - 1:T, :]
        m_new = m_t[T - 1:T, :]
        decay = jnp.exp(f_last + m_prev - m_new)
        w_end = jnp.exp(f_last - f_c + ig_c - m_new)
        wk = w_end * kf
        c_ref[h] = decay * c_old + _dot_tn(wk.astype(BF16), v)
        n_ref[h:h + 1, :] = decay * n_old + jnp.sum(wk, axis=0, keepdims=True)
        m_ref[h:h + 1, :] = jnp.broadcast_to(m_new, (1, N_GATE_LANES))
        y = hh * lax.rsqrt(jnp.mean(hh * hh, axis=-1, keepdims=True) + EPS)
        out = (y * ang_ref[:, cols]) * _sigmoid(ao[:, cols])
        hmix_ref[:, cols] = out.astype(BF16)
        yield

    qh = z[4] * _sigmoid(z[4])
    zf, iv, bg = z[5], z[6], z[7]
    lb = lb_ref[...]
    lf = jnp.log(lb + (1.0 - lb) * _sigmoid(zf))
    kk = (1.0 - lb) * _sigmoid(-zf)
    sh = int(math.log2(hchunk))
    same_chunk = lax.shift_right_logical(ri, sh) == lax.shift_right_logical(ci, sh)
    blocktril = jnp.where(causal & same_chunk, 1.0, 0.0).astype(BF16)
    G = sum(_dot(blocktril, p) for p in _split3(lf))
    q_dec = (qh * jnp.exp(G)).astype(BF16)
    k_inv = (kk * jnp.exp(-G)).astype(BF16)
    ivb = iv.astype(BF16)
    yield
    rs = lax.broadcasted_iota(jnp.int32, (hchunk, hchunk), 0)
    cs = lax.broadcasted_iota(jnp.int32, (hchunk, hchunk), 1)
    sub_causal = rs >= cs
    for j in range(T // hchunk):
        r = slice(j * hchunk, (j + 1) * hchunk)
        g_end = G[(j + 1) * hchunk - 1:(j + 1) * hchunk, :]
        k_end = (kk[r, :] * jnp.exp(g_end - G[r, :])).astype(BF16)
        e_end = jnp.exp(g_end)
        for h in range(HEADS):
            cols = slice(h * HEAD_DIM, (h + 1) * HEAD_DIM)
            qd = q_dec[r, cols]
            att = jnp.where(sub_causal, _dot_nt(qd, k_inv[r, cols]), 0.0)
            st_old = st_ref[h]
            o = _dot(att.astype(BF16), ivb[r, cols]) + _dot_nt(qd, st_old.astype(BF16))
            st_ref[h] = st_old * e_end[:, cols] + _dot_tn(ivb[r, cols], k_end[:, cols])
            y = o * lax.rsqrt(jnp.mean(o * o, axis=-1, keepdims=True) + EPS)
            gate = bg[r, cols]
            out = (y * bng_ref[:, cols]) * (gate * _sigmoid(gate))
            hmix_ref[r, W + h * HEAD_DIM:W + (h + 1) * HEAD_DIM] = out.astype(BF16)
            if h % 2 == 1:
                yield

    d_model = x.shape[-1]
    for k in range(d_model // OUT_COLS):
        cols = slice(k * OUT_COLS, (k + 1) * OUT_COLS)
        xo_ref[:, cols] = x[:, cols] + _dot(hmix_ref[...], wout_ref[:, cols])
        yield


def _mixer_ab(x, ng, w, c0, n0, m0, s0):
    B, L, D = x.shape
    T = min(MIX_ROWS, L)
    hchunk = HGRN_CHUNK if T % HGRN_CHUNK == 0 else T
    W = HEADS * HEAD_DIM
    nb = _seqs_per_step(B, T)
    kern = functools.partial(_mixer_ab_kernel, n_seq=nb, stagger=_stagger(T, AB_STAGGER), rows=T, hchunk=hchunk)
    state4 = pl.BlockSpec((nb, HEADS, HEAD_DIM, HEAD_DIM), lambda b, l: (b, 0, 0, 0))
    state_n = pl.BlockSpec((nb, HEADS, HEAD_DIM), lambda b, l: (b, 0, 0))
    state_m = pl.BlockSpec((nb, 8, N_GATE_LANES), lambda b, l: (b, 0, 0))
    xspec = pl.BlockSpec((nb, T, D), lambda b, l: (b, l, 0))
    consts = (ng, w["w_big"], w["b_big"], w["w_gc"], w["b_gc"],
              w["a_norm_g"], w["b_norm_g"], w["lb"], w["w_out"])
    assert len(consts) == N_AB_CONSTS
    return pl.pallas_call(
        kern,
        grid=(B // nb, L // T),
        in_specs=[xspec] + [_const_spec(c.shape) for c in consts] + [state4, state_n, state_m, state4],
        out_specs=[xspec, state4, state_n, state_m, state4],
        out_shape=[jax.ShapeDtypeStruct((B, L, D), F32),
                   jax.ShapeDtypeStruct((B, HEADS, HEAD_DIM, HEAD_DIM), F32),
                   jax.ShapeDtypeStruct((B, HEADS, HEAD_DIM), F32),
                   jax.ShapeDtypeStruct((B, 8, N_GATE_LANES), F32),
                   jax.ShapeDtypeStruct((B, HEADS, HEAD_DIM, HEAD_DIM), F32)],
        scratch_shapes=[pltpu.VMEM((nb, HEADS, HEAD_DIM, HEAD_DIM), F32),
                        pltpu.VMEM((nb, T, 2 * W), BF16)],
        compiler_params=_cparams(("parallel", "arbitrary")),
        name="mixer_ab",
    )(x, *consts, c0, n0, m0, s0)


N_CD_CONSTS = 24


def _mixer_cd_kernel(x_ref, *refs, n_seq, stagger, **dims):
    consts = refs[:N_CD_CONSTS]
    per_seq = refs[N_CD_CONSTS:]
    h0_ref, conv0_ref, sre0_ref, sim0_ref, _, h_ref, conv_ref, sre_ref, sim_ref = per_seq[:9]

    @pl.when(pl.program_id(1) == 0)
    def _():
        h_ref[...] = h0_ref[...]
        conv_ref[...] = conv0_ref[...]
        sre_ref[...] = sre0_ref[...]
        sim_ref[...] = sim0_ref[...]

    _run_staggered([_mixer_cd_one(x_ref.at[s], *consts, *[r.at[s] for r in per_seq], **dims)
                    for s in range(n_seq)], stagger)


def _mixer_cd_one(x_ref, ng_ref, perm_ref, permt_ref, wcd_ref, bcd_ref, cw_ref, cb_ref, wa_ref, ba_ref,
                  wx_ref, bx_ref, lam_ref, wbu_ref, abr_ref, abi_ref, pr_ref, pi_ref, pwr_ref, pwi_ref,
                  wc_ref, dsk_ref, wglu_ref, bglu_ref, wout_ref, h0_ref, conv0_ref, sre0_ref, sim0_ref,
                  xo_ref, h_ref, conv_ref, sre_ref, sim_ref, ext_ref, lre_ref, lim_ref, sb_ref, hmix_ref,
                  zg_ref, zd_ref, *, rows, c_width, d_width, n_state, gate_tile):
    T = rows
    n = T // 8
    log_n = int(math.log2(n))
    halo = (CONV_W - 1) * 8
    xn = _rms(x_ref[...], ng_ref[...]).astype(BF16)
    xp = _dot(perm_ref[...], xn).astype(BF16)
    yield

    for k in range((2 * c_width + d_width) // gate_tile):
        lo = k * gate_tile
        tile = _dot(xp, wcd_ref[:, lo:lo + gate_tile]) + bcd_ref[:, lo:lo + gate_tile]
        if lo < c_width:
            zg_ref[:, lo:lo + gate_tile] = tile
        elif lo < 2 * c_width:
            ext_ref[halo:halo + T, lo - c_width:lo - c_width + gate_tile] = tile
        else:
            zd_ref[:, lo - 2 * c_width:lo - 2 * c_width + gate_tile] = tile
        yield

    hist = conv_ref[...]
    sub_c = lax.broadcasted_iota(jnp.int32, (8, c_width), 0)
    for k in range(CONV_W - 1):
        g = n - (CONV_W - 1) + k
        grp = ext_ref[halo + g * 8:halo + (g + 1) * 8, :]
        ext_ref[k * 8:(k + 1) * 8, :] = jnp.where(sub_c == 0, hist[k:k + 1, :], pltpu.roll(grp, 1, 0))
        conv_ref[k:k + 1, :] = grp[7:8, :]
    u = cb_ref[...] + cw_ref[CONV_W - 1:CONV_W, :] * ext_ref[halo:halo + T, :]
    for k in range(CONV_W - 1):
        u = u + cw_ref[k:k + 1, :] * ext_ref[k * 8:k * 8 + T, :]
    yield

    ub = u.astype(BF16)
    log_sig_lam = _log_sigmoid(lam_ref[...])
    sub_t = lax.broadcasted_iota(jnp.int32, (8, gate_tile), 0)
    for k in range(c_width // gate_tile):
        cols = slice(k * gate_tile, (k + 1) * gate_tile)
        r = _sigmoid(_dot(ub[:, cols], wa_ref[k]) + ba_ref[:, cols])
        i = _sigmoid(_dot(ub[:, cols], wx_ref[k]) + bx_ref[:, cols])
        log_a = (RG_C * r) * log_sig_lam[:, cols]
        a = jnp.exp(log_a)
        b = jnp.sqrt(-jnp.tanh(log_a) * (a * a + 1.0)) * (i * u[:, cols])
        dec = a[0:8, :]
        loc = b[0:8, :]
        decs, locs = [dec], [loc]
        for j in range(1, n):
            aj = a[j * 8:(j + 1) * 8, :]
            loc = aj * loc + b[j * 8:(j + 1) * 8, :]
            dec = aj * dec
            decs.append(dec)
            locs.append(loc)
        h_prev = h_ref[:, cols]
        ea = dec
        eb = loc + jnp.where(sub_t == 0, dec * h_prev, 0.0)
        for d in (1, 2, 4):
            a_sh = jnp.where(sub_t >= d, pltpu.roll(ea, d, 0), 1.0)
            b_sh = jnp.where(sub_t >= d, pltpu.roll(eb, d, 0), 0.0)
            eb = ea * b_sh + eb
            ea = ea * a_sh
        carry = jnp.where(sub_t == 0, h_prev, pltpu.roll(eb, 1, 0))
        h_ref[:, cols] = eb[7:8, :]
        h = jnp.concatenate([locs[j] + decs[j] * carry for j in range(n)], axis=0)
        hmix_ref[:, cols] = (jax.nn.gelu(zg_ref[:, cols], approximate=True) * h).astype(BF16)
        yield

    bu = _dot(zd_ref[...].astype(BF16), wbu_ref[...])
    for lo in range(0, n_state, S5_LANES):
        re_cols = slice(lo, lo + S5_LANES)
        im_cols = slice(n_state + lo, n_state + lo + S5_LANES)
        p_re = jnp.broadcast_to(abr_ref[:, re_cols], (8, S5_LANES))
        p_im = jnp.broadcast_to(abi_ref[:, re_cols], (8, S5_LANES))
        l_re = bu[0:8, re_cols]
        l_im = bu[0:8, im_cols]
        lre_ref[0:8, re_cols] = l_re
        lim_ref[0:8, re_cols] = l_im
        for j in range(1, n):
            rows_j = slice(j * 8, (j + 1) * 8)
            l_re, l_im = (bu[rows_j, re_cols] + (p_re * l_re - p_im * l_im),
                          bu[rows_j, im_cols] + (p_re * l_im + p_im * l_re))
            lre_ref[rows_j, re_cols] = l_re
            lim_ref[rows_j, re_cols] = l_im
        yield
        sub_s = lax.broadcasted_iota(jnp.int32, (8, S5_LANES), 0)
        s_re = sre_ref[:, re_cols]
        s_im = sim_ref[:, re_cols]
        q_re = pr_ref[log_n:log_n + 1, re_cols]
        q_im = pi_ref[log_n:log_n + 1, re_cols]
        e_re = l_re + jnp.where(sub_s == 0, q_re * s_re - q_im * s_im, 0.0)
        e_im = l_im + jnp.where(sub_s == 0, q_re * s_im + q_im * s_re, 0.0)
        for k, d in enumerate((1, 2, 4)):
            q_re = pr_ref[log_n + k:log_n + k + 1, re_cols]
            q_im = pi_ref[log_n + k:log_n + k + 1, re_cols]
            r_sh = jnp.where(sub_s >= d, pltpu.roll(e_re, d, 0), 0.0)
            i_sh = jnp.where(sub_s >= d, pltpu.roll(e_im, d, 0), 0.0)
            e_re, e_im = e_re + (q_re * r_sh - q_im * i_sh), e_im + (q_re * i_sh + q_im * r_sh)
        c_re = jnp.where(sub_s == 0, s_re, pltpu.roll(e_re, 1, 0))
        c_im = jnp.where(sub_s == 0, s_im, pltpu.roll(e_im, 1, 0))
        sre_ref[:, re_cols] = e_re[7:8, :]
        sim_ref[:, re_cols] = e_im[7:8, :]
        for j in range(0, n, 2):
            f_re, f_im = [], []
            for jj in (j, j + 1):
                rows_j = slice(jj * 8, (jj + 1) * 8)
                w_re = pwr_ref[rows_j, re_cols]
                w_im = pwi_ref[rows_j, re_cols]
                f_re.append(lre_ref[rows_j, re_cols] + (w_re * c_re - w_im * c_im))
                f_im.append(lim_ref[rows_j, re_cols] + (w_re * c_im + w_im * c_re))
            sb_ref[j * 8:(j + 2) * 8, re_cols] = jnp.concatenate(f_re, axis=0).astype(BF16)
            sb_ref[j * 8:(j + 2) * 8, im_cols] = jnp.concatenate(f_im, axis=0).astype(BF16)
        yield
    y = _dot(sb_ref[...], wc_ref[...]) + dsk_ref[...] * zd_ref[...]
    y = jax.nn.gelu(y, approximate=True)
    y = y * _sigmoid(_dot(y.astype(BF16), wglu_ref[...]) + bglu_ref[...])
    hmix_ref[:, c_width:c_width + d_width] = y.astype(BF16)
    yield

    d_model = x_ref.shape[-1]
    mixed = []
    for k in range(d_model // gate_tile):
        cols = slice(k * gate_tile, (k + 1) * gate_tile)
        mixed.append(_dot(permt_ref[...], hmix_ref[:, cols]).astype(BF16))
        yield
    mixed = jnp.concatenate(mixed, axis=1)
    for k in range(d_model // gate_tile):
        cols = slice(k * gate_tile, (k + 1) * gate_tile)
        xo_ref[:, cols] = x_ref[:, cols] + _dot(mixed, wout_ref[:, cols])
        yield


def _mixer_cd(x, ng, w, h0, conv0, sre0, sim0):
    B, L, D = x.shape
    T = min(MIX_ROWS, L)
    c_width = h0.shape[-1]
    n_state = sre0.shape[-1]
    d_width = w["w_glu"].shape[0]
    nb = _seqs_per_step(B, T)
    kern = functools.partial(_mixer_cd_kernel, n_seq=nb, stagger=_stagger(T, CD_PHASE_PIECES), rows=T,
                             c_width=c_width, d_width=d_width,
                             n_state=n_state, gate_tile=w["w_a"].shape[-1])
    xspec = pl.BlockSpec((nb, T, D), lambda b, l: (b, l, 0))
    st_h = pl.BlockSpec((nb, 1, c_width), lambda b, l: (b, 0, 0))
    st_conv = pl.BlockSpec((nb, CONV_W - 1, c_width), lambda b, l: (b, 0, 0))
    st_s = pl.BlockSpec((nb, 1, n_state), lambda b, l: (b, 0, 0))
    perm, perm_t = _regroup_matrices(T)
    n_pow = (T // 8) * 8
    consts = (ng, perm, perm_t, w["w_cd"], w["b_cd"], w["conv_w"], w["conv_b"], w["w_a"], w["b_a"], w["w_x"],
              w["b_x"], w["lam"], w["w_bu"], w["ab_re"], w["ab_im"], w["pow_re"], w["pow_im"],
              w["run_re"][:n_pow], w["run_im"][:n_pow], w["w_c"], w["d_skip"], w["w_glu"], w["b_glu"], w["w_out"])
    assert len(consts) == N_CD_CONSTS
    return pl.pallas_call(
        kern,
        grid=(B // nb, L // T),
        in_specs=[xspec] + [_const_spec(c.shape) for c in consts] + [st_h, st_conv, st_s, st_s],
        out_specs=[xspec, st_h, st_conv, st_s, st_s],
        out_shape=[jax.ShapeDtypeStruct((B, L, D), F32),
                   jax.ShapeDtypeStruct((B, 1, c_width), F32),
                   jax.ShapeDtypeStruct((B, CONV_W - 1, c_width), F32),
                   jax.ShapeDtypeStruct((B, 1, n_state), F32),
                   jax.ShapeDtypeStruct((B, 1, n_state), F32)],
        scratch_shapes=[pltpu.VMEM((nb, T + (CONV_W - 1) * 8, c_width), F32),
                        pltpu.VMEM((nb, T, n_state), F32),
                        pltpu.VMEM((nb, T, n_state), F32),
                        pltpu.VMEM((nb, T, 2 * n_state), BF16),
                        pltpu.VMEM((nb, T, c_width + d_width), BF16),
                        pltpu.VMEM((nb, T, c_width), F32),
                        pltpu.VMEM((nb, T, d_width), F32)],
        compiler_params=_cparams(("parallel", "arbitrary")),
        name="mixer_cd",
    )(x, *consts, h0, conv0, sre0, sim0)


def _regroup_matrices(rows):
    n = rows // 8
    r = np.arange(rows)
    p = np.zeros((rows, rows), np.float32)
    p[r, (r % 8) * n + r // 8] = 1.0
    return jnp.asarray(p, BF16), jnp.asarray(p.T, BF16)


def _prep_s5_kernel(are_ref, aim_ref, ldt_ref, wbre_ref, wbim_ref, abr_ref, abi_ref, pr_ref, pi_ref,
                    runr_ref, runi_ref, wbu_ref, *, n_state):
    a_re = are_ref[...]
    a_im = aim_ref[...]
    dt = jnp.exp(ldt_ref[...])
    mag = jnp.exp(dt * a_re)
    ab_re = mag * jnp.cos(dt * a_im)
    ab_im = mag * jnp.sin(dt * a_im)
    inv = 1.0 / (a_re * a_re + a_im * a_im)
    z_re = ((ab_re - 1.0) * a_re + ab_im * a_im) * inv
    z_im = (ab_im * a_re - (ab_re - 1.0) * a_im) * inv
    abr_ref[...] = ab_re
    abi_ref[...] = ab_im
    wb_re = wbre_ref[...]
    wb_im = wbim_ref[...]
    wbu_ref[:, :n_state] = (z_re * wb_re - z_im * wb_im).astype(BF16)
    wbu_ref[:, n_state:] = (z_re * wb_im + z_im * wb_re).astype(BF16)
    p_re, p_im = ab_re, ab_im
    for k in range(MAX_SCAN_STEPS):
        pr_ref[k:k + 1, :] = p_re
        pi_ref[k:k + 1, :] = p_im
        p_re, p_im = p_re * p_re - p_im * p_im, 2.0 * (p_re * p_im)
    p_re, p_im = ab_re, ab_im
    for j in range(MIX_ROWS // 8):
        runr_ref[j * 8:(j + 1) * 8, :] = jnp.broadcast_to(p_re, (8, n_state))
        runi_ref[j * 8:(j + 1) * 8, :] = jnp.broadcast_to(p_im, (8, n_state))
        p_re, p_im = p_re * ab_re - p_im * ab_im, p_re * ab_im + p_im * ab_re


def _prep_s5(a_re, a_im, log_dt_rep, wb_re, wb_im):
    n_state = a_re.shape[-1]
    d_width = wb_re.shape[0]
    kern = functools.partial(_prep_s5_kernel, n_state=n_state)
    row = jax.ShapeDtypeStruct((1, n_state), F32)
    tab = jax.ShapeDtypeStruct((MAX_SCAN_STEPS, n_state), F32)
    run = jax.ShapeDtypeStruct((MIX_ROWS, n_state), F32)
    return pl.pallas_call(
        kern,
        out_shape=[row, row, tab, tab, run, run, jax.ShapeDtypeStruct((d_width, 2 * n_state), BF16)],
        name="prep_s5",
    )(a_re, a_im, log_dt_rep, wb_re, wb_im)


def _prep_lb_kernel(logits_ref, lb_ref):
    lg = logits_ref[...]
    n = lg.shape[0]
    mx = lg[0:1, :]
    for i in range(1, n):
        mx = jnp.maximum(mx, lg[i:i + 1, :])
    e = [jnp.exp(lg[i:i + 1, :] - mx) for i in range(n)]
    tot = e[0]
    for i in range(1, n):
        tot = tot + e[i]
    run = None
    for i in range(n):
        p = e[i] / tot
        run = p if run is None else run + p
        lb_ref[i:i + 1, :] = run


def _prep_lb(logits):
    return pl.pallas_call(_prep_lb_kernel, out_shape=jax.ShapeDtypeStruct(logits.shape, F32),
                          name="prep_lb")(logits)


def _block_diag(blocks):
    n, r, c = blocks.shape
    eye = jnp.eye(n, dtype=blocks.dtype)
    return (eye[:, None, :, None] * blocks[:, :, None, :]).reshape(n * r, n * c)


def kernel(x_prompt, x_sample, state_mlstm_C, state_mlstm_n, state_mlstm_m, state_hgrn_S, state_rglru_h,
           state_rglru_conv, state_s5_re, state_s5_im, norm_g, final_norm_g, ffn_w_gate, ffn_w_up, ffn_w_down,
           ab_w_in, ab_b_in, mlstm_norm_g, hgrn_norm_g, hgrn_lb_logits, ab_w_out, cd_w_in, cd_b_in, conv_w,
           conv_b, rg_w_a, rg_b_a, rg_w_x, rg_b_x, rg_lambda, s5_A_re, s5_A_im, s5_log_dt, s5_B_re, s5_B_im,
           s5_C_re, s5_C_im, s5_D, s5_w_glu, s5_b_glu, cd_w_out):
    depth = norm_g.shape[0]
    d_model = x_prompt.shape[-1]
    W = HEADS * HEAD_DIM
    c_width = rg_lambda.shape[-1]
    groups, n_st, s5_ch = s5_B_re.shape[1:]
    n_state = groups * n_st
    d_width = groups * s5_ch
    gate_tile = 256
    blocks_per_tile = gate_tile // rg_w_a.shape[-1]

    lb_all = _prep_lb(hgrn_lb_logits.astype(F32))

    ffn = []
    for l in range(depth):
        ffn.append([(ffn_w_gate[l, i].astype(BF16), ffn_w_up[l, i].astype(BF16), ffn_w_down[l, i].astype(BF16))
                    for i in range(2)])
    mix_w = []
    for l in range(depth):
        j = l // 2
        if l % 2 == 0:
            wi, bi = ab_w_in[j], ab_b_in[j]
            big = jnp.concatenate([wi[:, :4 * W], wi[:, 4 * W + 2 * HEADS:]], axis=1)
            b_big = jnp.concatenate([bi[:4 * W], bi[4 * W + 2 * HEADS:]])[None, :]
            wg = wi[:, 4 * W:4 * W + 2 * HEADS]
            bgate = bi[4 * W:4 * W + 2 * HEADS]
            mix_w.append(dict(
                w_big=big.astype(BF16), b_big=b_big,
                w_gc=jnp.pad(wg, ((0, 0), (0, N_GATE_LANES - 2 * HEADS))).astype(BF16),
                b_gc=jnp.pad(bgate, (0, N_GATE_LANES - 2 * HEADS))[None, :],
                a_norm_g=mlstm_norm_g[j][None, :], b_norm_g=hgrn_norm_g[j][None, :],
                lb=lb_all[l][None, :], w_out=ab_w_out[j].astype(BF16)))
        else:
            n_tiles = c_width // gate_tile

            def tiles(wb):
                wb = wb.reshape(n_tiles, blocks_per_tile, wb.shape[-2], wb.shape[-1])
                return jnp.stack([_block_diag(wb[t]) for t in range(n_tiles)]).astype(BF16)

            wb_re = _block_diag(jnp.swapaxes(s5_B_re[j], 1, 2))
            wb_im = _block_diag(jnp.swapaxes(s5_B_im[j], 1, 2))
            ab_re, ab_im, pow_re, pow_im, run_re, run_im, w_bu = _prep_s5(
                s5_A_re[j].reshape(1, n_state), s5_A_im[j].reshape(1, n_state),
                jnp.repeat(s5_log_dt[j], n_st)[None, :], wb_re, wb_im)
            w_c = jnp.concatenate([_block_diag(jnp.swapaxes(s5_C_re[j], 1, 2)),
                                   -_block_diag(jnp.swapaxes(s5_C_im[j], 1, 2))], axis=0)
            mix_w.append(dict(
                w_cd=cd_w_in[j].astype(BF16), b_cd=cd_b_in[j][None, :], conv_w=conv_w[j], conv_b=conv_b[j][None, :],
                w_a=tiles(rg_w_a[j]), b_a=rg_b_a[j][None, :], w_x=tiles(rg_w_x[j]), b_x=rg_b_x[j][None, :],
                lam=rg_lambda[j][None, :], w_bu=w_bu, ab_re=ab_re, ab_im=ab_im, pow_re=pow_re, pow_im=pow_im,
                run_re=run_re, run_im=run_im,
                w_c=w_c.astype(BF16), d_skip=s5_D[j][None, :], w_glu=s5_w_glu[j].astype(BF16),
                b_glu=s5_b_glu[j][None, :], w_out=cd_w_out[j].astype(BF16)))
    fg = final_norm_g[None, :]

    def run_trunk(x, states):
        mC, mn, mm, hS, rh, rc, sre, sim = states
        B, L, _ = x.shape
        outs = ([], [], [], [], [], [], [], [])
        for l in range(depth):
            j = l // 2
            wg, wu, wd = ffn[l][0]
            x = _ffn(x.reshape(B * L, d_model), norm_g[l, 0][None, :], wg, wu, wd, fg, False).reshape(B, L, d_model)
            if l % 2 == 0:
                m0 = jnp.broadcast_to(jnp.pad(mm[j], ((0, 0), (0, 8 - HEADS)))[:, :, None], (B, 8, N_GATE_LANES))
                x, c1, n1, m1, s1 = _mixer_ab(x, norm_g[l, 1][None, :], mix_w[l], mC[j], mn[j], m0, hS[j])
                for lst, val in zip(outs[:4], (c1, n1, m1[:, :HEADS, 0], s1)):
                    lst.append(val)
            else:
                x, h1, buf1, re1, im1 = _mixer_cd(x, norm_g[l, 1][None, :], mix_w[l], rh[j][:, None, :], rc[j],
                                                  sre[j].reshape(B, 1, n_state), sim[j].reshape(B, 1, n_state))
                for lst, val in zip(outs[4:], (h1[:, 0], buf1, re1.reshape(B, groups, n_st),
                                               im1.reshape(B, groups, n_st))):
                    lst.append(val)
            wg, wu, wd = ffn[l][1]
            x = _ffn(x.reshape(B * L, d_model), norm_g[l, 2][None, :], wg, wu, wd, fg,
                     l == depth - 1).reshape(B, L, d_model)
        return x, [jnp.stack(lst) for lst in outs]

    n_even = (depth + 1) // 2
    n_odd = depth // 2
    bp = x_prompt.shape[0]
    zero_states = (jnp.zeros((n_even, bp, HEADS, HEAD_DIM, HEAD_DIM), F32),
                   jnp.zeros((n_even, bp, HEADS, HEAD_DIM), F32),
                   jnp.zeros((n_even, bp, HEADS), F32),
                   jnp.zeros((n_even, bp, HEADS, HEAD_DIM, HEAD_DIM), F32),
                   jnp.zeros((n_odd, bp, c_width), F32),
                   jnp.zeros((n_odd, bp, CONV_W - 1, c_width), F32),
                   jnp.zeros((n_odd, bp, groups, n_st), F32),
                   jnp.zeros((n_odd, bp, groups, n_st), F32))
    y_prompt, p_st = run_trunk(x_prompt, zero_states)
    y_sample, s_st = run_trunk(x_sample, (state_mlstm_C, state_mlstm_n, state_mlstm_m, state_hgrn_S,
                                          state_rglru_h, state_rglru_conv, state_s5_re, state_s5_im))
    return (y_prompt, y_sample, *p_st, *s_st)
```

```python
import functools
import math

import jax
import jax.numpy as jnp
import numpy as np
from jax import lax
from jax.experimental import pallas as pl
from jax.experimental.pallas import tpu as pltpu

F32 = jnp.float32
BF16 = jnp.bfloat16

EPS = 1e-6
RG_C = 8.0
CONV_W = 4
HEADS = 4
HEAD_DIM = 128
N_GATE_LANES = 128
V7X_VMEM_LIMIT = 56 * 1024 * 1024
FFN_COLS = 256
FFN_ROWS = 512
FFN_TILES = 2
FFN_STAGGER = 6
MIX_ROWS = 256
HGRN_CHUNK = 64
MAX_SCAN_STEPS = 8
AB_SEQS = 4
AB_STAGGER = 9
OUT_COLS = 256
MIX_SEQS = 4
S5_LANES = 1024
CD_PHASE_PIECES = 8


def _dot(a, b):
    return jnp.dot(a, b, preferred_element_type=F32)


def _dot_nt(a, b):
    return lax.dot_general(a, b, (((1,), (1,)), ((), ())), preferred_element_type=F32)


def _dot_tn(a, b):
    return lax.dot_general(a, b, (((0,), (0,)), ((), ())), preferred_element_type=F32)


def _rms(x, g):
    y = x * lax.rsqrt(jnp.mean(x * x, axis=-1, keepdims=True) + EPS)
    return y * g


def _sigmoid(x):
    return jax.nn.sigmoid(x)


def _log_sigmoid(x):
    return jnp.minimum(x, 0.0) - jnp.log1p(jnp.exp(-jnp.abs(x)))


def _split3(a):
    hi = a.astype(BF16)
    r = a - hi.astype(F32)
    mid = r.astype(BF16)
    lo = (r - mid.astype(F32)).astype(BF16)
    return hi, mid, lo


def _const_spec(shape):
    nd = len(shape)
    return pl.BlockSpec(shape, lambda *_: (0,) * nd, pipeline_mode=pl.Buffered(1))


def _run_staggered(piece_generators, offset):
    pending = list(piece_generators)
    live = []
    tick = 0
    while pending or live:
        if pending and tick % offset == 0:
            live.append(pending.pop(0))
        for g in list(live):
            try:
                next(g)
            except StopIteration:
                live.remove(g)
        tick += 1


def _cparams(sem):
    return pltpu.CompilerParams(dimension_semantics=sem, vmem_limit_bytes=V7X_VMEM_LIMIT)


def _ffn_kernel(x_ref, g_ref, wg_ref, wu_ref, wd_ref, fg_ref, o_ref, hid_ref, *, n_tiles, rows, **kw):
    tiles = [slice(t * rows, (t + 1) * rows) for t in range(n_tiles)]
    _run_staggered([_ffn_tile(x_ref.at[r], g_ref, wg_ref, wu_ref, wd_ref, fg_ref, o_ref.at[r], hid_ref.at[r], **kw)
                    for r in tiles], FFN_STAGGER)


def _ffn_tile(x_ref, g_ref, wg_ref, wu_ref, wd_ref, fg_ref, o_ref, hid_ref, *, n_chunks, final_norm):
    xn = _rms(x_ref[...], g_ref[...]).astype(BF16)
    yield
    for j in range(n_chunks):
        cols = slice(j * FFN_COLS, (j + 1) * FFN_COLS)
        gate = _dot(xn, wg_ref[:, cols])
        up = _dot(xn, wu_ref[:, cols])
        hid_ref[:, cols] = ((gate * _sigmoid(gate)) * up).astype(BF16)
        yield
    y = x_ref[...] + 0.5 * _dot(hid_ref[...], wd_ref[...])
    if final_norm:
        y = _rms(y, fg_ref[...])
    o_ref[...] = y


def _ffn(x2, g, wg, wu, wd, fg, final_norm):
    n_tok, d = x2.shape
    rows = min(FFN_ROWS, n_tok)
    n_tiles = FFN_TILES if n_tok % (FFN_TILES * rows) == 0 else 1
    tm = n_tiles * rows
    n_chunks = wg.shape[1] // FFN_COLS
    kern = functools.partial(_ffn_kernel, n_tiles=n_tiles, rows=rows, n_chunks=n_chunks, final_norm=final_norm)
    return pl.pallas_call(
        kern,
        grid=(n_tok // tm,),
        in_specs=[pl.BlockSpec((tm, d), lambda i: (i, 0)),
                  _const_spec(g.shape), _const_spec(wg.shape), _const_spec(wu.shape),
                  _const_spec(wd.shape), _const_spec(fg.shape)],
        out_specs=pl.BlockSpec((tm, d), lambda i: (i, 0)),
        out_shape=jax.ShapeDtypeStruct((n_tok, d), F32),
        scratch_shapes=[pltpu.VMEM((tm, wg.shape[1]), BF16)],
        compiler_params=_cparams(("parallel",)),
        name="ffn",
    )(x2, g, wg, wu, wd, fg)


N_AB_CONSTS = 9


def _mixer_ab_kernel(x_ref, *refs, n_seq, **dims):
    consts = refs[:N_AB_CONSTS]
    per_seq = refs[N_AB_CONSTS:]
    c0_ref, n0_ref, m0_ref, s0_ref, _, c_ref, n_ref, m_ref, s_ref, st_ref = per_seq[:10]
    li = pl.program_id(1)

    @pl.when(li == 0)
    def _():
        c_ref[...] = c0_ref[...]
        n_ref[...] = n0_ref[...]
        m_ref[...] = m0_ref[...]
        for s in range(n_seq):
            for h in range(HEADS):
                st_ref[s, h] = s0_ref[s, h].T

    _run_staggered([_mixer_ab_one(x_ref.at[s], *consts, *[r.at[s] for r in per_seq], **dims)
                    for s in range(n_seq)], AB_STAGGER)

    @pl.when(li == pl.num_programs(1) - 1)
    def _():
        for s in range(n_seq):
            for h in range(HEADS):
                s_ref[s, h] = st_ref[s, h].T


def _mixer_ab_one(x_ref, ng_ref, wbig_ref, bbig_ref, wgc_ref, bgc_ref,
                  ang_ref, bng_ref, lb_ref, wout_ref, c0_ref, n0_ref, m0_ref, s0_ref,
                  xo_ref, c_ref, n_ref, m_ref, s_ref, st_ref, hmix_ref, *, rows, hchunk):
    T = rows
    W = HEADS * HEAD_DIM
    x = x_ref[...]
    xn = _rms(x, ng_ref[...]).astype(BF16)
    zc = _dot(xn, wgc_ref[...]) + bgc_ref[...]
    yield

    z = []
    for k in range(8):
        halves = []
        for c0 in (k * W, k * W + W // 2):
            halves.append(_dot(xn, wbig_ref[:, c0:c0 + W // 2]) + bbig_ref[:, c0:c0 + W // 2])
            yield
        z.append(jnp.concatenate(halves, axis=1))

    aq = z[0] * (HEAD_DIM ** -0.5)
    ak, av, ao = z[1], z[2], z[3]
    ri = lax.broadcasted_iota(jnp.int32, (T, T), 0)
    ci = lax.broadcasted_iota(jnp.int32, (T, T), 1)
    causal = ri >= ci
    tril = jnp.where(causal, 1.0, 0.0).astype(BF16)
    fc = sum(_dot(tril, p) for p in _split3(_log_sigmoid(zc)))
    zr = zc.T
    fr = fc.T
    yield
    for h in range(HEADS):
        cols = slice(h * HEAD_DIM, (h + 1) * HEAD_DIM)
        ig_c = zc[:, h:h + 1]
        ig_r = zr[h:h + 1, :]
        f_c = fc[:, HEADS + h:HEADS + h + 1]
        f_r = fr[HEADS + h:HEADS + h + 1, :]
        m_prev = m_ref[h:h + 1, 0:1]
        log_src = jnp.where(causal, f_c + (ig_r - f_r), -jnp.inf)
        log_prev = f_c + m_prev
        m_t = jnp.maximum(log_prev, jnp.max(log_src, axis=-1, keepdims=True))
        w_src = jnp.exp(log_src - m_t)
        w_prev = jnp.exp(log_prev - m_t)
        qf = aq[:, cols]
        kf = ak[:, cols]
        q = qf.astype(BF16)
        v = av[:, cols].astype(BF16)
        s = _dot_nt(q, kf.astype(BF16)) * w_src
        yield
        c_old = c_ref[h]
        n_old = n_ref[h:h + 1, :]
        num = _dot(s.astype(BF16), v) + w_prev * _dot(q, c_old.astype(BF16))
        den = jnp.sum(s, axis=-1, keepdims=True) + w_prev * jnp.sum(qf * n_old, axis=-1, keepdims=True)
        den = jnp.maximum(jnp.abs(den), jnp.exp(-m_t))
        hh = num / den
        f_last = f_c[T - 1:T, :]
        m_new = m_t[T - 1:T, :]
        decay = jnp.exp(f_last + m_prev - m_new)
        w_end = jnp.exp(f_last - f_c + ig_c - m_new)
        wk = w_end * kf
        c_ref[h] = decay * c_old + _dot_tn(wk.astype(BF16), v)
        n_ref[h:h + 1, :] = decay * n_old + jnp.sum(wk, axis=0, keepdims=True)
        m_ref[h:h + 1, :] = jnp.broadcast_to(m_new, (1, N_GATE_LANES))
        y = hh * lax.rsqrt(jnp.mean(hh * hh, axis=-1, keepdims=True) + EPS)
        out = (y * ang_ref[:, cols]) * _sigmoid(ao[:, cols])
        hmix_ref[:, cols] = out.astype(BF16)
        yield

    qh = z[4] * _sigmoid(z[4])
    zf, iv, bg = z[5], z[6], z[7]
    lb = lb_ref[...]
    lf = jnp.log(lb + (1.0 - lb) * _sigmoid(zf))
    kk = (1.0 - lb) * _sigmoid(-zf)
    sh = int(math.log2(hchunk))
    same_chunk = lax.shift_right_logical(ri, sh) == lax.shift_right_logical(ci, sh)
    blocktril = jnp.where(causal & same_chunk, 1.0, 0.0).astype(BF16)
    G = sum(_dot(blocktril, p) for p in _split3(lf))
    q_dec = (qh * jnp.exp(G)).astype(BF16)
    k_inv = (kk * jnp.exp(-G)).astype(BF16)
    ivb = iv.astype(BF16)
    yield
    rs = lax.broadcasted_iota(jnp.int32, (hchunk, hchunk), 0)
    cs = lax.broadcasted_iota(jnp.int32, (hchunk, hchunk), 1)
    sub_causal = rs >= cs
    for j in range(T // hchunk):
        r = slice(j * hchunk, (j + 1) * hchunk)
        g_end = G[(j + 1) * hchunk - 1:(j + 1) * hchunk, :]
        k_end = (kk[r, :] * jnp.exp(g_end - G[r, :])).astype(BF16)
        e_end = jnp.exp(g_end)
        for h in range(HEADS):
            cols = slice(h * HEAD_DIM, (h + 1) * HEAD_DIM)
            qd = q_dec[r, cols]
            att = jnp.where(sub_causal, _dot_nt(qd, k_inv[r, cols]), 0.0)
            st_old = st_ref[h]
            o = _dot(att.astype(BF16), ivb[r, cols]) + _dot_nt(qd, st_old.astype(BF16))
            st_ref[h] = st_old * e_end[:, cols] + _dot_tn(ivb[r, cols], k_end[:, cols])
            y = o * lax.rsqrt(jnp.mean(o * o, axis=-1, keepdims=True) + EPS)
            gate = bg[r, cols]
            out = (y * bng_ref[:, cols]) * (gate * _sigmoid(gate))
            hmix_ref[r, W + h * HEAD_DIM:W + (h + 1) * HEAD_DIM] = out.astype(BF16)
            if h % 2 == 1:
                yield

    d_model = x.shape[-1]
    for k in range(d_model // OUT_COLS):
        cols = slice(k * OUT_COLS, (k + 1) * OUT_COLS)
        xo_ref[:, cols] = x[:, cols] + _dot(hmix_ref[...], wout_ref[:, cols])
        yield


def _mixer_ab(x, ng, w, c0, n0, m0, s0):
    B, L, D = x.shape
    T = min(MIX_ROWS, L)
    hchunk = HGRN_CHUNK if T % HGRN_CHUNK == 0 else T
    W = HEADS * HEAD_DIM
    nb = AB_SEQS if B % AB_SEQS == 0 else 1
    kern = functools.partial(_mixer_ab_kernel, n_seq=nb, rows=T, hchunk=hchunk)
    state4 = pl.BlockSpec((nb, HEADS, HEAD_DIM, HEAD_DIM), lambda b, l: (b, 0, 0, 0))
    state_n = pl.BlockSpec((nb, HEADS, HEAD_DIM), lambda b, l: (b, 0, 0))
    state_m = pl.BlockSpec((nb, 8, N_GATE_LANES), lambda b, l: (b, 0, 0))
    xspec = pl.BlockSpec((nb, T, D), lambda b, l: (b, l, 0))
    consts = (ng, w["w_big"], w["b_big"], w["w_gc"], w["b_gc"],
              w["a_norm_g"], w["b_norm_g"], w["lb"], w["w_out"])
    assert len(consts) == N_AB_CONSTS
    return pl.pallas_call(
        kern,
        grid=(B // nb, L // T),
        in_specs=[xspec] + [_const_spec(c.shape) for c in consts] + [state4, state_n, state_m, state4],
        out_specs=[xspec, state4, state_n, state_m, state4],
        out_shape=[jax.ShapeDtypeStruct((B, L, D), F32),
                   jax.ShapeDtypeStruct((B, HEADS, HEAD_DIM, HEAD_DIM), F32),
                   jax.ShapeDtypeStruct((B, HEADS, HEAD_DIM), F32),
                   jax.ShapeDtypeStruct((B, 8, N_GATE_LANES), F32),
                   jax.ShapeDtypeStruct((B, HEADS, HEAD_DIM, HEAD_DIM), F32)],
        scratch_shapes=[pltpu.VMEM((nb, HEADS, HEAD_DIM, HEAD_DIM), F32),
                        pltpu.VMEM((nb, T, 2 * W), BF16)],
        compiler_params=_cparams(("parallel", "arbitrary")),
        name="mixer_ab",
    )(x, *consts, c0, n0, m0, s0)


N_CD_CONSTS = 24


def _mixer_cd_kernel(x_ref, *refs, n_seq, **dims):
    consts = refs[:N_CD_CONSTS]
    per_seq = refs[N_CD_CONSTS:]
    h0_ref, conv0_ref, sre0_ref, sim0_ref, _, h_ref, conv_ref, sre_ref, sim_ref = per_seq[:9]

    @pl.when(pl.program_id(1) == 0)
    def _():
        h_ref[...] = h0_ref[...]
        conv_ref[...] = conv0_ref[...]
        sre_ref[...] = sre0_ref[...]
        sim_ref[...] = sim0_ref[...]

    _run_staggered([_mixer_cd_one(x_ref.at[s], *consts, *[r.at[s] for r in per_seq], **dims)
                    for s in range(n_seq)], CD_PHASE_PIECES)


def _mixer_cd_one(x_ref, ng_ref, perm_ref, permt_ref, wcd_ref, bcd_ref, cw_ref, cb_ref, wa_ref, ba_ref,
                  wx_ref, bx_ref, lam_ref, wbu_ref, abr_ref, abi_ref, pr_ref, pi_ref, pwr_ref, pwi_ref,
                  wc_ref, dsk_ref, wglu_ref, bglu_ref, wout_ref, h0_ref, conv0_ref, sre0_ref, sim0_ref,
                  xo_ref, h_ref, conv_ref, sre_ref, sim_ref, ext_ref, lre_ref, lim_ref, sb_ref, hmix_ref,
                  zg_ref, zd_ref, *, rows, c_width, d_width, n_state, gate_tile):
    T = rows
    n = T // 8
    log_n = int(math.log2(n))
    halo = (CONV_W - 1) * 8
    xn = _rms(x_ref[...], ng_ref[...]).astype(BF16)
    xp = _dot(perm_ref[...], xn).astype(BF16)
    yield

    for k in range((2 * c_width + d_width) // gate_tile):
        lo = k * gate_tile
        tile = _dot(xp, wcd_ref[:, lo:lo + gate_tile]) + bcd_ref[:, lo:lo + gate_tile]
        if lo < c_width:
            zg_ref[:, lo:lo + gate_tile] = tile
        elif lo < 2 * c_width:
            ext_ref[halo:halo + T, lo - c_width:lo - c_width + gate_tile] = tile
        else:
            zd_ref[:, lo - 2 * c_width:lo - 2 * c_width + gate_tile] = tile
        yield

    hist = conv_ref[...]
    sub_c = lax.broadcasted_iota(jnp.int32, (8, c_width), 0)
    for k in range(CONV_W - 1):
        g = n - (CONV_W - 1) + k
        grp = ext_ref[halo + g * 8:halo + (g + 1) * 8, :]
        ext_ref[k * 8:(k + 1) * 8, :] = jnp.where(sub_c == 0, hist[k:k + 1, :], pltpu.roll(grp, 1, 0))
        conv_ref[k:k + 1, :] = grp[7:8, :]
    u = cb_ref[...] + cw_ref[CONV_W - 1:CONV_W, :] * ext_ref[halo:halo + T, :]
    for k in range(CONV_W - 1):
        u = u + cw_ref[k:k + 1, :] * ext_ref[k * 8:k * 8 + T, :]
    yield

    ub = u.astype(BF16)
    log_sig_lam = _log_sigmoid(lam_ref[...])
    sub_t = lax.broadcasted_iota(jnp.int32, (8, gate_tile), 0)
    for k in range(c_width // gate_tile):
        cols = slice(k * gate_tile, (k + 1) * gate_tile)
        r = _sigmoid(_dot(ub[:, cols], wa_ref[k]) + ba_ref[:, cols])
        i = _sigmoid(_dot(ub[:, cols], wx_ref[k]) + bx_ref[:, cols])
        log_a = (RG_C * r) * log_sig_lam[:, cols]
        a = jnp.exp(log_a)
        b = jnp.sqrt(-jnp.tanh(log_a) * (a * a + 1.0)) * (i * u[:, cols])
        dec = a[0:8, :]
        loc = b[0:8, :]
        decs, locs = [dec], [loc]
        for j in range(1, n):
            aj = a[j * 8:(j + 1) * 8, :]
            loc = aj * loc + b[j * 8:(j + 1) * 8, :]
            dec = aj * dec
            decs.append(dec)
            locs.append(loc)
        h_prev = h_ref[:, cols]
        ea = dec
        eb = loc + jnp.where(sub_t == 0, dec * h_prev, 0.0)
        for d in (1, 2, 4):
            a_sh = jnp.where(sub_t >= d, pltpu.roll(ea, d, 0), 1.0)
            b_sh = jnp.where(sub_t >= d, pltpu.roll(eb, d, 0), 0.0)
            eb = ea * b_sh + eb
            ea = ea * a_sh
        carry = jnp.where(sub_t == 0, h_prev, pltpu.roll(eb, 1, 0))
        h_ref[:, cols] = eb[7:8, :]
        h = jnp.concatenate([locs[j] + decs[j] * carry for j in range(n)], axis=0)
        hmix_ref[:, cols] = (jax.nn.gelu(zg_ref[:, cols], approximate=True) * h).astype(BF16)
        yield

    bu = _dot(zd_ref[...].astype(BF16), wbu_ref[...])
    for lo in range(0, n_state, S5_LANES):
        re_cols = slice(lo, lo + S5_LANES)
        im_cols = slice(n_state + lo, n_state + lo + S5_LANES)
        p_re = jnp.broadcast_to(abr_ref[:, re_cols], (8, S5_LANES))
        p_im = jnp.broadcast_to(abi_ref[:, re_cols], (8, S5_LANES))
        l_re = bu[0:8, re_cols]
        l_im = bu[0:8, im_cols]
        lre_ref[0:8, re_cols] = l_re
        lim_ref[0:8, re_cols] = l_im
        for j in range(1, n):
            rows_j = slice(j * 8, (j + 1) * 8)
            l_re, l_im = (bu[rows_j, re_cols] + (p_re * l_re - p_im * l_im),
                          bu[rows_j, im_cols] + (p_re * l_im + p_im * l_re))
            lre_ref[rows_j, re_cols] = l_re
            lim_ref[rows_j, re_cols] = l_im
        yield
        sub_s = lax.broadcasted_iota(jnp.int32, (8, S5_LANES), 0)
        s_re = sre_ref[:, re_cols]
        s_im = sim_ref[:, re_cols]
        q_re = pr_ref[log_n:log_n + 1, re_cols]
        q_im = pi_ref[log_n:log_n + 1, re_cols]
        e_re = l_re + jnp.where(sub_s == 0, q_re * s_re - q_im * s_im, 0.0)
        e_im = l_im + jnp.where(sub_s == 0, q_re * s_im + q_im * s_re, 0.0)
        for k, d in enumerate((1, 2, 4)):
            q_re = pr_ref[log_n + k:log_n + k + 1, re_cols]
            q_im = pi_ref[log_n + k:log_n + k + 1, re_cols]
            r_sh = jnp.where(sub_s >= d, pltpu.roll(e_re, d, 0), 0.0)
            i_sh = jnp.where(sub_s >= d, pltpu.roll(e_im, d, 0), 0.0)
            e_re, e_im = e_re + (q_re * r_sh - q_im * i_sh), e_im + (q_re * i_sh + q_im * r_sh)
        c_re = jnp.where(sub_s == 0, s_re, pltpu.roll(e_re, 1, 0))
        c_im = jnp.where(sub_s == 0, s_im, pltpu.roll(e_im, 1, 0))
        sre_ref[:, re_cols] = e_re[7:8, :]
        sim_ref[:, re_cols] = e_im[7:8, :]
        for j in range(0, n, 2):
            f_re, f_im = [], []
            for jj in (j, j + 1):
                rows_j = slice(jj * 8, (jj + 1) * 8)
                w_re = pwr_ref[rows_j, re_cols]
                w_im = pwi_ref[rows_j, re_cols]
                f_re.append(lre_ref[rows_j, re_cols] + (w_re * c_re - w_im * c_im))
                f_im.append(lim_ref[rows_j, re_cols] + (w_re * c_im + w_im * c_re))
            sb_ref[j * 8:(j + 2) * 8, re_cols] = jnp.concatenate(f_re, axis=0).astype(BF16)
            sb_ref[j * 8:(j + 2) * 8, im_cols] = jnp.concatenate(f_im, axis=0).astype(BF16)
        yield
    y = _dot(sb_ref[...], wc_ref[...]) + dsk_ref[...] * zd_ref[...]
    y = jax.nn.gelu(y, approximate=True)
    y = y * _sigmoid(_dot(y.astype(BF16), wglu_ref[...]) + bglu_ref[...])
    hmix_ref[:, c_width:c_width + d_width] = y.astype(BF16)
    yield

    d_model = x_ref.shape[-1]
    mixed = []
    for k in range(d_model // gate_tile):
        cols = slice(k * gate_tile, (k + 1) * gate_tile)
        mixed.append(_dot(permt_ref[...], hmix_ref[:, cols]).astype(BF16))
        yield
    mixed = jnp.concatenate(mixed, axis=1)
    for k in range(d_model // gate_tile):
        cols = slice(k * gate_tile, (k + 1) * gate_tile)
        xo_ref[:, cols] = x_ref[:, cols] + _dot(mixed, wout_ref[:, cols])
        yield


def _mixer_cd(x, ng, w, h0, conv0, sre0, sim0):
    B, L, D = x.shape
    T = min(MIX_ROWS, L)
    c_width = h0.shape[-1]
    n_state = sre0.shape[-1]
    d_width = w["w_glu"].shape[0]
    nb = MIX_SEQS if B % MIX_SEQS == 0 else 1
    kern = functools.partial(_mixer_cd_kernel, n_seq=nb, rows=T, c_width=c_width, d_width=d_width,
                             n_state=n_state, gate_tile=w["w_a"].shape[-1])
    xspec = pl.BlockSpec((nb, T, D), lambda b, l: (b, l, 0))
    st_h = pl.BlockSpec((nb, 1, c_width), lambda b, l: (b, 0, 0))
    st_conv = pl.BlockSpec((nb, CONV_W - 1, c_width), lambda b, l: (b, 0, 0))
    st_s = pl.BlockSpec((nb, 1, n_state), lambda b, l: (b, 0, 0))
    perm, perm_t = _regroup_matrices(T)
    n_pow = (T // 8) * 8
    consts = (ng, perm, perm_t, w["w_cd"], w["b_cd"], w["conv_w"], w["conv_b"], w["w_a"], w["b_a"], w["w_x"],
              w["b_x"], w["lam"], w["w_bu"], w["ab_re"], w["ab_im"], w["pow_re"], w["pow_im"],
              w["run_re"][:n_pow], w["run_im"][:n_pow], w["w_c"], w["d_skip"], w["w_glu"], w["b_glu"], w["w_out"])
    assert len(consts) == N_CD_CONSTS
    return pl.pallas_call(
        kern,
        grid=(B // nb, L // T),
        in_specs=[xspec] + [_const_spec(c.shape) for c in consts] + [st_h, st_conv, st_s, st_s],
        out_specs=[xspec, st_h, st_conv, st_s, st_s],
        out_shape=[jax.ShapeDtypeStruct((B, L, D), F32),
                   jax.ShapeDtypeStruct((B, 1, c_width), F32),
                   jax.ShapeDtypeStruct((B, CONV_W - 1, c_width), F32),
                   jax.ShapeDtypeStruct((B, 1, n_state), F32),
                   jax.ShapeDtypeStruct((B, 1, n_state), F32)],
        scratch_shapes=[pltpu.VMEM((nb, T + (CONV_W - 1) * 8, c_width), F32),
                        pltpu.VMEM((nb, T, n_state), F32),
                        pltpu.VMEM((nb, T, n_state), F32),
                        pltpu.VMEM((nb, T, 2 * n_state), BF16),
                        pltpu.VMEM((nb, T, c_width + d_width), BF16),
                        pltpu.VMEM((nb, T, c_width), F32),
                        pltpu.VMEM((nb, T, d_width), F32)],
        compiler_params=_cparams(("parallel", "arbitrary")),
        name="mixer_cd",
    )(x, *consts, h0, conv0, sre0, sim0)


def _regroup_matrices(rows):
    n = rows // 8
    r = np.arange(rows)
    p = np.zeros((rows, rows), np.float32)
    p[r, (r % 8) * n + r // 8] = 1.0
    return jnp.asarray(p, BF16), jnp.asarray(p.T, BF16)


def _prep_s5_kernel(are_ref, aim_ref, ldt_ref, wbre_ref, wbim_ref, abr_ref, abi_ref, pr_ref, pi_ref,
                    runr_ref, runi_ref, wbu_ref, *, n_state):
    a_re = are_ref[...]
    a_im = aim_ref[...]
    dt = jnp.exp(ldt_ref[...])
    mag = jnp.exp(dt * a_re)
    ab_re = mag * jnp.cos(dt * a_im)
    ab_im = mag * jnp.sin(dt * a_im)
    inv = 1.0 / (a_re * a_re + a_im * a_im)
    z_re = ((ab_re - 1.0) * a_re + ab_im * a_im) * inv
    z_im = (ab_im * a_re - (ab_re - 1.0) * a_im) * inv
    abr_ref[...] = ab_re
    abi_ref[...] = ab_im
    wb_re = wbre_ref[...]
    wb_im = wbim_ref[...]
    wbu_ref[:, :n_state] = (z_re * wb_re - z_im * wb_im).astype(BF16)
    wbu_ref[:, n_state:] = (z_re * wb_im + z_im * wb_re).astype(BF16)
    p_re, p_im = ab_re, ab_im
    for k in range(MAX_SCAN_STEPS):
        pr_ref[k:k + 1, :] = p_re
        pi_ref[k:k + 1, :] = p_im
        p_re, p_im = p_re * p_re - p_im * p_im, 2.0 * (p_re * p_im)
    p_re, p_im = ab_re, ab_im
    for j in range(MIX_ROWS // 8):
        runr_ref[j * 8:(j + 1) * 8, :] = jnp.broadcast_to(p_re, (8, n_state))
        runi_ref[j * 8:(j + 1) * 8, :] = jnp.broadcast_to(p_im, (8, n_state))
        p_re, p_im = p_re * ab_re - p_im * ab_im, p_re * ab_im + p_im * ab_re


def _prep_s5(a_re, a_im, log_dt_rep, wb_re, wb_im):
    n_state = a_re.shape[-1]
    d_width = wb_re.shape[0]
    kern = functools.partial(_prep_s5_kernel, n_state=n_state)
    row = jax.ShapeDtypeStruct((1, n_state), F32)
    tab = jax.ShapeDtypeStruct((MAX_SCAN_STEPS, n_state), F32)
    run = jax.ShapeDtypeStruct((MIX_ROWS, n_state), F32)
    return pl.pallas_call(
        kern,
        out_shape=[row, row, tab, tab, run, run, jax.ShapeDtypeStruct((d_width, 2 * n_state), BF16)],
        name="prep_s5",
    )(a_re, a_im, log_dt_rep, wb_re, wb_im)


def _prep_lb_kernel(logits_ref, lb_ref):
    lg = logits_ref[...]
    n = lg.shape[0]
    mx = lg[0:1, :]
    for i in range(1, n):
        mx = jnp.maximum(mx, lg[i:i + 1, :])
    e = [jnp.exp(lg[i:i + 1, :] - mx) for i in range(n)]
    tot = e[0]
    for i in range(1, n):
        tot = tot + e[i]
    run = None
    for i in range(n):
        p = e[i] / tot
        run = p if run is None else run + p
        lb_ref[i:i + 1, :] = run


def _prep_lb(logits):
    return pl.pallas_call(_prep_lb_kernel, out_shape=jax.ShapeDtypeStruct(logits.shape, F32),
                          name="prep_lb")(logits)


def _block_diag(blocks):
    n, r, c = blocks.shape
    eye = jnp.eye(n, dtype=blocks.dtype)
    return (eye[:, None, :, None] * blocks[:, :, None, :]).reshape(n * r, n * c)


def kernel(x_prompt, x_sample, state_mlstm_C, state_mlstm_n, state_mlstm_m, state_hgrn_S, state_rglru_h,
           state_rglru_conv, state_s5_re, state_s5_im, norm_g, final_norm_g, ffn_w_gate, ffn_w_up, ffn_w_down,
           ab_w_in, ab_b_in, mlstm_norm_g, hgrn_norm_g, hgrn_lb_logits, ab_w_out, cd_w_in, cd_b_in, conv_w,
           conv_b, rg_w_a, rg_b_a, rg_w_x, rg_b_x, rg_lambda, s5_A_re, s5_A_im, s5_log_dt, s5_B_re, s5_B_im,
           s5_C_re, s5_C_im, s5_D, s5_w_glu, s5_b_glu, cd_w_out):
    depth = norm_g.shape[0]
    d_model = x_prompt.shape[-1]
    W = HEADS * HEAD_DIM
    c_width = rg_lambda.shape[-1]
    groups, n_st, s5_ch = s5_B_re.shape[1:]
    n_state = groups * n_st
    d_width = groups * s5_ch
    gate_tile = 256
    blocks_per_tile = gate_tile // rg_w_a.shape[-1]

    lb_all = _prep_lb(hgrn_lb_logits.astype(F32))

    ffn = []
    for l in range(depth):
        ffn.append([(ffn_w_gate[l, i].astype(BF16), ffn_w_up[l, i].astype(BF16), ffn_w_down[l, i].astype(BF16))
                    for i in range(2)])
    mix_w = []
    for l in range(depth):
        j = l // 2
        if l % 2 == 0:
            wi, bi = ab_w_in[j], ab_b_in[j]
            big = jnp.concatenate([wi[:, :4 * W], wi[:, 4 * W + 2 * HEADS:]], axis=1)
            b_big = jnp.concatenate([bi[:4 * W], bi[4 * W + 2 * HEADS:]])[None, :]
            wg = wi[:, 4 * W:4 * W + 2 * HEADS]
            bgate = bi[4 * W:4 * W + 2 * HEADS]
            mix_w.append(dict(
                w_big=big.astype(BF16), b_big=b_big,
                w_gc=jnp.pad(wg, ((0, 0), (0, N_GATE_LANES - 2 * HEADS))).astype(BF16),
                b_gc=jnp.pad(bgate, (0, N_GATE_LANES - 2 * HEADS))[None, :],
                a_norm_g=mlstm_norm_g[j][None, :], b_norm_g=hgrn_norm_g[j][None, :],
                lb=lb_all[l][None, :], w_out=ab_w_out[j].astype(BF16)))
        else:
            n_tiles = c_width // gate_tile

            def tiles(wb):
                wb = wb.reshape(n_tiles, blocks_per_tile, wb.shape[-2], wb.shape[-1])
                return jnp.stack([_block_diag(wb[t]) for t in range(n_tiles)]).astype(BF16)

            wb_re = _block_diag(jnp.swapaxes(s5_B_re[j], 1, 2))
            wb_im = _block_diag(jnp.swapaxes(s5_B_im[j], 1, 2))
            ab_re, ab_im, pow_re, pow_im, run_re, run_im, w_bu = _prep_s5(
                s5_A_re[j].reshape(1, n_state), s5_A_im[j].reshape(1, n_state),
                jnp.repeat(s5_log_dt[j], n_st)[None, :], wb_re, wb_im)
            w_c = jnp.concatenate([_block_diag(jnp.swapaxes(s5_C_re[j], 1, 2)),
                                   -_block_diag(jnp.swapaxes(s5_C_im[j], 1, 2))], axis=0)
            mix_w.append(dict(
                w_cd=cd_w_in[j].astype(BF16), b_cd=cd_b_in[j][None, :], conv_w=conv_w[j], conv_b=conv_b[j][None, :],
                w_a=tiles(rg_w_a[j]), b_a=rg_b_a[j][None, :], w_x=tiles(rg_w_x[j]), b_x=rg_b_x[j][None, :],
                lam=rg_lambda[j][None, :], w_bu=w_bu, ab_re=ab_re, ab_im=ab_im, pow_re=pow_re, pow_im=pow_im,
                run_re=run_re, run_im=run_im,
                w_c=w_c.astype(BF16), d_skip=s5_D[j][None, :], w_glu=s5_w_glu[j].astype(BF16),
                b_glu=s5_b_glu[j][None, :], w_out=cd_w_out[j].astype(BF16)))
    fg = final_norm_g[None, :]

    def run_trunk(x, states):
        mC, mn, mm, hS, rh, rc, sre, sim = states
        B, L, _ = x.shape
        outs = ([], [], [], [], [], [], [], [])
        for l in range(depth):
            j = l // 2
            wg, wu, wd = ffn[l][0]
            x = _ffn(x.reshape(B * L, d_model), norm_g[l, 0][None, :], wg, wu, wd, fg, False).reshape(B, L, d_model)
            if l % 2 == 0:
                m0 = jnp.broadcast_to(jnp.pad(mm[j], ((0, 0), (0, 8 - HEADS)))[:, :, None], (B, 8, N_GATE_LANES))
                x, c1, n1, m1, s1 = _mixer_ab(x, norm_g[l, 1][None, :], mix_w[l], mC[j], mn[j], m0, hS[j])
                for lst, val in zip(outs[:4], (c1, n1, m1[:, :HEADS, 0], s1)):
                    lst.append(val)
            else:
                x, h1, buf1, re1, im1 = _mixer_cd(x, norm_g[l, 1][None, :], mix_w[l], rh[j][:, None, :], rc[j],
                                                  sre[j].reshape(B, 1, n_state), sim[j].reshape(B, 1, n_state))
                for lst, val in zip(outs[4:], (h1[:, 0], buf1, re1.reshape(B, groups, n_st),
                                               im1.reshape(B, groups, n_st))):
                    lst.append(val)
            wg, wu, wd = ffn[l][1]
            x = _ffn(x.reshape(B * L, d_model), norm_g[l, 2][None, :], wg, wu, wd, fg,
                     l == depth - 1).reshape(B, L, d_model)
        return x, [jnp.stack(lst) for lst in outs]

    n_even = (depth + 1) // 2
    n_odd = depth // 2
    bp = x_prompt.shape[0]
    zero_states = (jnp.zeros((n_even, bp, HEADS, HEAD_DIM, HEAD_DIM), F32),
                   jnp.zeros((n_even, bp, HEADS, HEAD_DIM), F32),
                   jnp.zeros((n_even, bp, HEADS), F32),
                   jnp.zeros((n_even, bp, HEADS, HEAD_DIM, HEAD_DIM), F32),
                   jnp.zeros((n_odd, bp, c_width), F32),
                   jnp.zeros((n_odd, bp, CONV_W - 1, c_width), F32),
                   jnp.zeros((n_odd, bp, groups, n_st), F32),
                   jnp.zeros((n_odd, bp, groups, n_st), F32))
    y_prompt, p_st = run_trunk(x_prompt, zero_states)
    y_sample, s_st = run_trunk(x_sample, (state_mlstm_C, state_mlstm_n, state_mlstm_m, state_hgrn_S,
                                          state_rglru_h, state_rglru_conv, state_s5_re, state_s5_im))
    return (y_prompt, y_sample, *p_st, *s_st)
```
